```python
import math
import jax, jax.numpy as jnp
from jax import lax
import numpy as np

D_MODEL = 1024
BATCH = 4
SEQ = 4096
DEPTH = 4
DEC_BATCH = 4
DEC_SEQ = 8192
PAST_LEN = 128

GRID_W = 64
HEAD_DIM = 64
N_HEADS_DIFF = 4
N_HEADS_NA = 8
W_DIFF = N_HEADS_DIFF * 2 * HEAD_DIM
W_NA = N_HEADS_NA * HEAD_DIM
WIN_H_MAX = 8
WIN_W = 16
D_FF = 2816
ROPE_THETA = 10000.0
EPS = 1e-6
Q_BLOCK = 128
IN_COLS = 3 * W_DIFF + 3 * W_NA + 2 * D_MODEL
SPLITS = (W_DIFF, 2 * W_DIFF, 3 * W_DIFF, 3 * W_DIFF + W_NA, 3 * W_DIFF + 2 * W_NA,
          3 * W_DIFF + 3 * W_NA, 3 * W_DIFF + 3 * W_NA + D_MODEL)

kernel_name = "hybrid_diffattn_natten_convffn_encoder"


def rms_norm(x, g):
    xf = x.astype(jnp.float32)
    y = xf * lax.rsqrt(jnp.mean(xf * xf, axis=-1, keepdims=True) + EPS)
    return (y * g.astype(jnp.float32)).astype(x.dtype)


def rope_tables(seq):
    half = HEAD_DIM // 2
    inv = jnp.power(ROPE_THETA, -jnp.arange(half, dtype=jnp.float32) * 2.0 / HEAD_DIM)
    ang = jnp.arange(seq, dtype=jnp.float32)[:, None] * inv[None, :]
    return jnp.cos(ang)[:, None, None, :], jnp.sin(ang)[:, None, None, :]


def apply_rope(x, cos, sin):
    half = HEAD_DIM // 2
    xf = x.astype(jnp.float32)
    x1, x2 = xf[..., :half], xf[..., half:]
    return jnp.concatenate([x1 * cos - x2 * sin, x1 * sin + x2 * cos], axis=-1).astype(x.dtype)


def diff_attention(q, k, v, lam):
    b, s = q.shape[0], q.shape[1]
    nb = s // Q_BLOCK
    scale = HEAD_DIM ** -0.5
    q_blocks = jnp.moveaxis(q.reshape(b, nb, Q_BLOCK, 2, N_HEADS_DIFF, HEAD_DIM), 1, 0)

    def one_block(qb):
        sc = jnp.einsum('bqchd,bkchd->bchqk', qb, k).astype(jnp.float32) * scale
        p = jax.nn.softmax(sc, axis=-1)
        a = p[:, 0] - lam * p[:, 1]
        return jnp.einsum('bhqk,bkhe->bqhe', a.astype(v.dtype), v)

    o = lax.map(one_block, q_blocks)
    return jnp.moveaxis(o, 0, 1).reshape(b, s, N_HEADS_DIFF, 2 * HEAD_DIM)


def neighborhood_attention(q, k, v, bias_table):
    b, s = q.shape[0], q.shape[1]
    rows = s // GRID_W
    kh = min(WIN_H_MAX, rows)
    shp = (b, rows, GRID_W, N_HEADS_NA, HEAD_DIM)
    qg, kg, vg = q.reshape(shp), k.reshape(shp), v.reshape(shp)
    cols = jnp.arange(GRID_W)
    col_start = jnp.clip(cols - WIN_W // 2, 0, GRID_W - WIN_W)
    col_idx = col_start[:, None] + jnp.arange(WIN_W)[None, :]
    dc = col_idx - cols[:, None] + (WIN_W - 1)
    scale = HEAD_DIM ** -0.5

    def one_row(r):
        rs = jnp.clip(r - kh // 2, 0, rows - kh)
        dr = rs + jnp.arange(kh) - r + (WIN_H_MAX - 1)
        bias = bias_table[:, dr[None, :, None], dc[:, None, :]]
        qr = lax.dynamic_index_in_dim(qg, r, axis=1, keepdims=False)
        kr = lax.dynamic_slice_in_dim(kg, rs, kh, axis=1)
        vr = lax.dynamic_slice_in_dim(vg, rs, kh, axis=1)
        kw = kr[:, :, col_idx]
        vw = vr[:, :, col_idx]
        sc = jnp.einsum('bchd,bicjhd->bhcij', qr, kw).astype(jnp.float32) * scale + bias.astype(jnp.float32)
        p = jax.nn.softmax(sc.reshape(b, N_HEADS_NA, GRID_W, kh * WIN_W), axis=-1)
        p = p.reshape(b, N_HEADS_NA, GRID_W, kh, WIN_W).astype(v.dtype)
        return jnp.einsum('bhcij,bicjhd->bchd', p, vw)

    o = lax.map(one_row, jnp.arange(rows))
    return jnp.moveaxis(o, 0, 1).reshape(b, s, W_NA)


def conv_gated_mlp(h, w_up, conv_w, conv_b, w_down):
    u = h @ w_up
    up = jnp.pad(u, ((0, 0), (1, 1), (0, 0)))
    c = up[:, :-2] * conv_w[0] + up[:, 1:-1] * conv_w[1] + up[:, 2:] * conv_w[2] + conv_b
    gate, val = c[..., :D_FF], c[..., D_FF:]
    return (jax.nn.gelu(gate) * val) @ w_down


def encoder_layer(x, l, cos, sin, g_mix, w_in, gq_a, gk_a, lam_q1, lam_k1, lam_q2, lam_k2,
                  g_sub, gq_n, gk_n, na_bias, w_pa, w_pb, w_o, g_ffn, w_up, conv_w, conv_b, w_down):
    b, s, _ = x.shape
    h = rms_norm(x, g_mix)
    z = h @ w_in
    qa, ka, va, qn, kn, vn, ga, gb = jnp.split(z, SPLITS, axis=-1)
    lambda_init = 0.8 - 0.6 * math.exp(-0.3 * l)
    qa = apply_rope(rms_norm(qa.reshape(b, s, 2, N_HEADS_DIFF, HEAD_DIM), gq_a), cos, sin)
    ka = apply_rope(rms_norm(ka.reshape(b, s, 2, N_HEADS_DIFF, HEAD_DIM), gk_a), cos, sin)
    va = va.reshape(b, s, N_HEADS_DIFF, 2 * HEAD_DIM)
    f32 = jnp.float32
    lam = (jnp.exp(jnp.sum(lam_q1.astype(f32) * lam_k1.astype(f32)))
           - jnp.exp(jnp.sum(lam_q2.astype(f32) * lam_k2.astype(f32))) + lambda_init)
    oa = diff_attention(qa, ka, va, lam)
    oa = (rms_norm(oa, g_sub) * (1.0 - lambda_init)).reshape(b, s, W_DIFF)
    qn = rms_norm(qn.reshape(b, s, N_HEADS_NA, HEAD_DIM), gq_n)
    kn = rms_norm(kn.reshape(b, s, N_HEADS_NA, HEAD_DIM), gk_n)
    vn = vn.reshape(b, s, N_HEADS_NA, HEAD_DIM)
    on = neighborhood_attention(qn, kn, vn, na_bias)
    mixed = jax.nn.sigmoid(ga) * (oa @ w_pa) + jax.nn.sigmoid(gb) * (on @ w_pb)
    x = x + mixed @ w_o
    x = x + conv_gated_mlp(rms_norm(x, g_ffn), w_up, conv_w, conv_b, w_down)
    return x


def trunk(x, g_mix, w_in, gq_a, gk_a, lam_q1, lam_k1, lam_q2, lam_k2, g_sub, gq_n, gk_n,
          na_bias, w_pa, w_pb, w_o, g_ffn, w_up, conv_w, conv_b, w_down):
    cos, sin = rope_tables(x.shape[1])
    for l in range(DEPTH):
        x = encoder_layer(x, l, cos, sin, g_mix[l], w_in[l], gq_a[l], gk_a[l], lam_q1[l], lam_k1[l],
                          lam_q2[l], lam_k2[l], g_sub[l], gq_n[l], gk_n[l], na_bias[l], w_pa[l],
                          w_pb[l], w_o[l], g_ffn[l], w_up[l], conv_w[l], conv_b[l], w_down[l])
    return x


def setup_inputs(seed: int = 0) -> dict:
    key = jax.random.key(seed)
    ks = jax.random.split(key, 24)
    nrm = lambda k, shp, sc: jax.random.normal(k, shp, jnp.float32) * sc
    gain = lambda k, shp: 1.0 + 0.02 * jax.random.normal(k, shp, jnp.float32)
    return {
        "x_prompt": nrm(ks[0], (BATCH, SEQ, D_MODEL), 1.0),
        "x_sample": nrm(ks[1], (DEC_BATCH, DEC_SEQ, D_MODEL), 1.0),
        "g_mix": gain(ks[2], (DEPTH, D_MODEL)),
        "w_in": nrm(ks[3], (DEPTH, D_MODEL, IN_COLS), D_MODEL ** -0.5),
        "gq_a": gain(ks[4], (DEPTH, HEAD_DIM)),
        "gk_a": gain(ks[5], (DEPTH, HEAD_DIM)),
        "lam_q1": nrm(ks[6], (DEPTH, HEAD_DIM), 0.1),
        "lam_k1": nrm(ks[7], (DEPTH, HEAD_DIM), 0.1),
        "lam_q2": nrm(ks[8], (DEPTH, HEAD_DIM), 0.1),
        "lam_k2": nrm(ks[9], (DEPTH, HEAD_DIM), 0.1),
        "g_sub": gain(ks[10], (DEPTH, 2 * HEAD_DIM)),
        "gq_n": gain(ks[11], (DEPTH, HEAD_DIM)),
        "gk_n": gain(ks[12], (DEPTH, HEAD_DIM)),
        "na_bias": nrm(ks[13], (DEPTH, N_HEADS_NA, 2 * WIN_H_MAX - 1, 2 * WIN_W - 1), 0.1),
        "w_pa": nrm(ks[14], (DEPTH, W_DIFF, D_MODEL), W_DIFF ** -0.5),
        "w_pb": nrm(ks[15], (DEPTH, W_NA, D_MODEL), W_NA ** -0.5),
        "w_o": nrm(ks[16], (DEPTH, D_MODEL, D_MODEL), D_MODEL ** -0.5),
        "g_ffn": gain(ks[17], (DEPTH, D_MODEL)),
        "w_up": nrm(ks[18], (DEPTH, D_MODEL, 2 * D_FF), D_MODEL ** -0.5),
        "conv_w": nrm(ks[19], (DEPTH, 3, 2 * D_FF), 3 ** -0.5),
        "conv_b": nrm(ks[20], (DEPTH, 2 * D_FF), 0.02),
        "w_down": nrm(ks[21], (DEPTH, D_FF, D_MODEL), D_FF ** -0.5),
    }


def reference(x_prompt, x_sample, g_mix, w_in, gq_a, gk_a, lam_q1, lam_k1, lam_q2, lam_k2, g_sub,
              gq_n, gk_n, na_bias, w_pa, w_pb, w_o, g_ffn, w_up, conv_w, conv_b, w_down):
    y_prompt = trunk(x_prompt, g_mix, w_in, gq_a, gk_a, lam_q1, lam_k1, lam_q2, lam_k2, g_sub, gq_n,
                     gk_n, na_bias, w_pa, w_pb, w_o, g_ffn, w_up, conv_w, conv_b, w_down)
    y_sample = trunk(x_sample, g_mix, w_in, gq_a, gk_a, lam_q1, lam_k1, lam_q2, lam_k2, g_sub, gq_n,
                     gk_n, na_bias, w_pa, w_pb, w_o, g_ffn, w_up, conv_w, conv_b, w_down)
    return (y_prompt, y_sample)
```

```python
import functools
import math

import numpy as np
import jax
import jax.numpy as jnp
from jax import lax
from jax.experimental import pallas as pl
from jax.experimental.pallas import tpu as pltpu

D_MODEL = 1024
HEAD_DIM = 64
N_HEADS_DIFF = 4
N_HEADS_NA = 8
W_DIFF = N_HEADS_DIFF * 2 * HEAD_DIM
W_NA = N_HEADS_NA * HEAD_DIM
ATT_COLS = 3 * W_DIFF + 3 * W_NA
GRID_W = 64
WIN_H = 8
WIN_W = 16
D_FF = 2816
ROPE_THETA = 10000.0
EPS = 1e-6
NEG = -1e30

NA_QROWS = 4
NA_KROWS = 12
FFN_CHUNKS = (768, 768, 768, 512)
HALO = 8

VMEM_LIMIT = 56 * 1024 * 1024

F32 = jnp.float32
BF16 = jnp.bfloat16


def _const_spec(shape):
    nd = len(shape)
    return pl.BlockSpec(shape, lambda *_: (0,) * nd, pipeline_mode=pl.Buffered(1))


def _params(sem):
    return pltpu.CompilerParams(dimension_semantics=sem, vmem_limit_bytes=VMEM_LIMIT)


def _rms(x, g):
    return x * lax.rsqrt(jnp.mean(x * x, axis=-1, keepdims=True) + EPS) * g


def _proj_kernel(x_ref, gmix_ref, w_ref, gqa_ref, gka_ref, gqn_ref, gkn_ref, cos_ref, sin_ref, gsum_ref,
                 qt_ref, k_ref, vt_ref, qn_ref, kn_ref, vn_ref):
    x = x_ref[0]
    h = _rms(x, gmix_ref[...]).astype(BF16)
    gsum = gsum_ref[...]

    def proj(c0):
        return jnp.dot(h, w_ref[:, c0:c0 + W_DIFF], preferred_element_type=F32)

    def head_norm(z, g):
        z2 = z * z
        hi = z2.astype(BF16)
        lo = (z2 - hi.astype(F32)).astype(BF16)
        parts = []
        for c in range(0, W_DIFF, 256):
            parts.append(jnp.dot(hi[:, c:c + 256], gsum, preferred_element_type=F32)
                         + jnp.dot(lo[:, c:c + 256], gsum, preferred_element_type=F32))
        ss = jnp.concatenate(parts, axis=-1)
        return z * lax.rsqrt(ss * (1.0 / HEAD_DIM) + EPS) * g

    cos = jnp.concatenate([cos_ref[...]] * 4, axis=-1)
    sin = jnp.concatenate([sin_ref[...]] * 4, axis=-1)
    lane = lax.broadcasted_iota(jnp.int32, (1, W_DIFF), 1)
    first_half = (lane % HEAD_DIM) < (HEAD_DIM // 2)

    def rope(y):
        partner = jnp.where(first_half, pltpu.roll(y, W_DIFF - HEAD_DIM // 2, 1), pltpu.roll(y, HEAD_DIM // 2, 1))
        return y * cos + partner * sin

    scale = HEAD_DIM ** -0.5
    qa = rope(head_norm(proj(0), gqa_ref[...])) * scale
    qt_ref[0] = qa.T.astype(BF16)
    ka = rope(head_norm(proj(W_DIFF), gka_ref[...]))
    k_ref[0] = ka.astype(BF16)
    vt_ref[0] = proj(2 * W_DIFF).T.astype(BF16)
    qn_ref[0] = (head_norm(proj(3 * W_DIFF), gqn_ref[...]) * scale).astype(BF16)
    kn_ref[0] = head_norm(proj(3 * W_DIFF + W_NA), gkn_ref[...]).astype(BF16)
    vn_ref[0] = proj(3 * W_DIFF + 2 * W_NA).astype(BF16)


def _proj(x, gmix, w_att, gqa, gka, gqn, gkn, cos, sin, gsum, tm):
    b, s, _ = x.shape
    tok = lambda i, j: (i, j, 0)
    tr = lambda i, j: (i, 0, j)
    row_major = jax.ShapeDtypeStruct((b, s, W_DIFF), BF16)
    col_major = jax.ShapeDtypeStruct((b, W_DIFF, s), BF16)
    return pl.pallas_call(
        _proj_kernel,
        grid=(b, s // tm),
        in_specs=[
            pl.BlockSpec((1, tm, D_MODEL), tok),
            _const_spec((1, D_MODEL)),
            _const_spec((D_MODEL, ATT_COLS)),
            _const_spec((1, W_DIFF)), _const_spec((1, W_DIFF)), _const_spec((1, W_NA)), _const_spec((1, W_NA)),
            pl.BlockSpec((tm, 128), lambda i, j: (j, 0)),
            pl.BlockSpec((tm, 128), lambda i, j: (j, 0)),
            _const_spec((256, 256)),
        ],
        out_specs=[
            pl.BlockSpec((1, W_DIFF, tm), tr),
            pl.BlockSpec((1, tm, W_DIFF), tok),
            pl.BlockSpec((1, W_DIFF, tm), tr),
            pl.BlockSpec((1, tm, W_NA), tok),
            pl.BlockSpec((1, tm, W_NA), tok),
            pl.BlockSpec((1, tm, W_NA), tok),
        ],
        out_shape=[col_major, row_major, col_major, row_major, row_major, row_major],
        compiler_params=_params(("parallel", "parallel")),
        name="proj",
    )(x, gmix, w_att, gqa, gka, gqn, gkn, cos, sin, gsum)


N_MAPS = 2 * N_HEADS_DIFF


def _diff_kernel(qt_ref, k_ref, vt_ref, lq1_ref, lk1_ref, lq2_ref, lk2_ref, gsub_ref, o_ref,
                 qz_sc, m_sc, l_sc, acc_sc, *, lambda_init):
    kv = pl.program_id(2)
    tq = qt_ref.shape[2]

    @pl.when(kv == 0)
    def _init():
        m_sc[...] = jnp.full(m_sc.shape, NEG, F32)
        l_sc[...] = jnp.zeros(l_sc.shape, F32)
        acc_sc[...] = jnp.zeros(acc_sc.shape, F32)
        row = lax.broadcasted_iota(jnp.int32, (128, tq), 0)
        for j in range(N_MAPS):
            grp = qt_ref[0, 128 * (j // 2):128 * (j // 2) + 128, :]
            keep = (row >= HEAD_DIM) if (j % 2) else (row < HEAD_DIM)
            qz_sc[j] = jnp.where(keep, grp, jnp.zeros_like(grp))

    for j in range(N_MAPS):
        h = j % N_HEADS_DIFF
        kp = k_ref[0, :, 128 * (j // 2):128 * (j // 2) + 128]
        s = jnp.dot(kp, qz_sc[j], preferred_element_type=F32)
        m_prev = m_sc[j]
        m_cur = jnp.maximum(m_prev, jnp.max(s, axis=0, keepdims=True))
        alpha = jnp.exp(m_prev - m_cur)
        p = jnp.exp(s - m_cur)
        l_sc[j] = alpha * l_sc[j] + jnp.sum(p, axis=0, keepdims=True)
        vt = vt_ref[0, 128 * h:128 * h + 128, :]
        acc_sc[j] = alpha * acc_sc[j] + jnp.dot(vt, p.astype(BF16), preferred_element_type=F32)
        m_sc[j] = m_cur

    @pl.when(kv == pl.num_programs(2) - 1)
    def _finish():
        lam = (jnp.exp(jnp.sum(lq1_ref[...] * lk1_ref[...])) - jnp.exp(jnp.sum(lq2_ref[...] * lk2_ref[...]))
               + lambda_init)
        for h in range(N_HEADS_DIFF):
            o0 = acc_sc[h] / l_sc[h]
            o1 = acc_sc[N_HEADS_DIFF + h] / l_sc[N_HEADS_DIFF + h]
            o = o0 - lam * o1
            o = o * lax.rsqrt(jnp.mean(o * o, axis=0, keepdims=True) + EPS) * gsub_ref[...] * (1.0 - lambda_init)
            o_ref[0, :, 128 * h:128 * h + 128] = o.T.astype(o_ref.dtype)


def _diff_attention(qt, k, vt, lq1, lk1, lq2, lk2, gsub, lambda_init, tq, tk):
    b, _, s = qt.shape
    return pl.pallas_call(
        functools.partial(_diff_kernel, lambda_init=lambda_init),
        grid=(b, s // tq, s // tk),
        in_specs=[
            pl.BlockSpec((1, W_DIFF, tq), lambda i, j, t: (i, 0, j)),
            pl.BlockSpec((1, tk, W_DIFF), lambda i, j, t: (i, t, 0)),
            pl.BlockSpec((1, W_DIFF, tk), lambda i, j, t: (i, 0, t)),
            _const_spec((1, HEAD_DIM)), _const_spec((1, HEAD_DIM)), _const_spec((1, HEAD_DIM)),
            _const_spec((1, HEAD_DIM)),
            _const_spec((2 * HEAD_DIM, 1)),
        ],
        out_specs=pl.BlockSpec((1, tq, W_DIFF), lambda i, j, t: (i, j, 0)),
        out_shape=jax.ShapeDtypeStruct((b, s, W_DIFF), BF16),
        scratch_shapes=[
            pltpu.VMEM((N_MAPS, 128, tq), BF16),
            pltpu.VMEM((N_MAPS, 1, tq), F32),
            pltpu.VMEM((N_MAPS, 1, tq), F32),
            pltpu.VMEM((N_MAPS, 128, tq), F32),
        ],
        compiler_params=_params(("parallel", "parallel", "arbitrary")),
        name="diff_attn",
    )(qt, k, vt, lq1, lk1, lq2, lk2, gsub)


def _na_selectors(rows):
    kh = min(WIN_H, rows)
    assert kh == WIN_H and rows >= NA_KROWS + NA_QROWS
    rsel = np.zeros((3, NA_QROWS, NA_KROWS, 2 * WIN_H - 1), np.float32)
    for v in range(3):
        r0 = (0, NA_QROWS, rows - NA_QROWS)[v]
        ws = min(max(r0 - 4, 0), rows - NA_KROWS)
        for t in range(NA_QROWS):
            r = r0 + t
            rs = min(max(r - kh // 2, 0), rows - kh)
            for i in range(NA_KROWS):
                kr = ws + i
                if rs <= kr < rs + kh:
                    rsel[v, t, i, kr - r + WIN_H - 1] = 1.0
    csel = np.zeros((GRID_W, GRID_W, 2 * WIN_W - 1), np.float32)
    for c in range(GRID_W):
        cs = min(max(c - WIN_W // 2, 0), GRID_W - WIN_W)
        for j in range(cs, cs + WIN_W):
            csel[c, j, j - c + WIN_W - 1] = 1.0
    return rsel, csel


def _na_bias_blocks(na_bias, rows):
    rsel, csel = _na_selectors(rows)
    dense = jnp.einsum("lhde,vtid,cje->lvhtcij", na_bias.astype(F32), rsel, csel,
                       precision=lax.Precision.HIGHEST)
    valid = np.einsum("vtid,cje->vtcij", rsel, csel) > 0.5
    dense = jnp.where(valid[None, :, None], dense, NEG)
    depth = na_bias.shape[0]
    return dense.reshape(depth, 3, N_HEADS_NA, NA_QROWS * GRID_W, NA_KROWS * GRID_W)


def _na_kernel(q_ref, k0_ref, k1_ref, k2_ref, v0_ref, v1_ref, v2_ref, bias_ref, o_ref):
    q = q_ref[0]
    kwin = jnp.concatenate([k0_ref[0], k1_ref[0], k2_ref[0]], axis=0)
    vwin = jnp.concatenate([v0_ref[0], v1_ref[0], v2_ref[0]], axis=0)
    outs = []
    for h in range(N_HEADS_NA):
        sl = slice(HEAD_DIM * h, HEAD_DIM * (h + 1))
        s = lax.dot_general(q[:, sl], kwin[:, sl], (((1,), (1,)), ((), ())), preferred_element_type=F32)
        s = s + bias_ref[0, h]
        p = jnp.exp(s - jnp.max(s, axis=-1, keepdims=True))
        l = jnp.sum(p, axis=-1, keepdims=True)
        outs.append(jnp.dot(p.astype(BF16), vwin[:, sl], preferred_element_type=F32) / l)
    o_ref[0] = jnp.concatenate(outs, axis=-1).astype(o_ref.dtype)


def _na_attention(qn, kn, vn, bias_blocks):
    b, s, _ = qn.shape
    rows = s // GRID_W
    nblk = rows // NA_QROWS
    tq = NA_QROWS * GRID_W

    def win(d):
        return lambda i, j: (i, jnp.clip(j - 1, 0, nblk - 3) + d, 0)

    variant = lambda i, j: (jnp.where(j == 0, 0, jnp.where(j == nblk - 1, 2, 1)), 0, 0, 0)
    blk = lambda f: pl.BlockSpec((1, tq, W_NA), f)
    return pl.pallas_call(
        _na_kernel,
        grid=(b, nblk),
        in_specs=[blk(lambda i, j: (i, j, 0)),
                  blk(win(0)), blk(win(1)), blk(win(2)),
                  blk(win(0)), blk(win(1)), blk(win(2)),
                  pl.BlockSpec((1, N_HEADS_NA, tq, NA_KROWS * GRID_W), variant)],
        out_specs=blk(lambda i, j: (i, j, 0)),
        out_shape=jax.ShapeDtypeStruct((b, s, W_NA), BF16),
        compiler_params=_params(("parallel", "arbitrary")),
        name="natt",
    )(qn, kn, kn, kn, vn, vn, vn, bias_blocks)


def _merge_kernel(x_ref, oa_ref, on_ref, gmix_ref, wg_ref, wpa_ref, wpb_ref, wo_ref, y_ref):
    x = x_ref[0]
    h = _rms(x, gmix_ref[...]).astype(BF16)
    ga = jnp.dot(h, wg_ref[:, :D_MODEL], preferred_element_type=F32)
    gb = jnp.dot(h, wg_ref[:, D_MODEL:], preferred_element_type=F32)
    pa = jnp.dot(oa_ref[0], wpa_ref[...], preferred_element_type=F32)
    pb = jnp.dot(on_ref[0], wpb_ref[...], preferred_element_type=F32)
    mixed = jax.nn.sigmoid(ga) * pa + jax.nn.sigmoid(gb) * pb
    y_ref[0] = x + jnp.dot(mixed.astype(BF16), wo_ref[...], preferred_element_type=F32)


def _merge(x, oa, on, gmix, w_gate, w_pa, w_pb, w_o, tm):
    b, s, _ = x.shape
    tok = lambda i, j: (i, j, 0)
    return pl.pallas_call(
        _merge_kernel,
        grid=(b, s // tm),
        in_specs=[
            pl.BlockSpec((1, tm, D_MODEL), tok),
            pl.BlockSpec((1, tm, W_DIFF), tok),
            pl.BlockSpec((1, tm, W_NA), tok),
            _const_spec((1, D_MODEL)),
            _const_spec((D_MODEL, 2 * D_MODEL)),
            _const_spec((W_DIFF, D_MODEL)),
            _const_spec((W_NA, D_MODEL)),
            _const_spec((D_MODEL, D_MODEL)),
        ],
        out_specs=pl.BlockSpec((1, tm, D_MODEL), tok),
        out_shape=jax.ShapeDtypeStruct(x.shape, F32),
        compiler_params=_params(("parallel", "parallel")),
        name="merge",
    )(x, oa, on, gmix, w_gate, w_pa, w_pb, w_o)


def _ffn_kernel(x_ref, xp_ref, xn_ref, g_ref, wup_ref, cw_ref, cb_ref, wdn_ref, y_ref):
    j = pl.program_id(1)
    tm = x_ref.shape[1]
    x = x_ref[0]
    xe = jnp.concatenate([xp_ref[0], x, xn_ref[0]], axis=0)
    row = lax.broadcasted_iota(jnp.int32, (tm + 2 * HALO, 1), 0)
    inside = jnp.logical_and(jnp.logical_or(row >= HALO, j > 0),
                             jnp.logical_or(row < tm + HALO, j < pl.num_programs(1) - 1))
    he = jnp.where(inside, _rms(xe, g_ref[...]), 0.0).astype(BF16)

    def conv(c0, n):
        u = jnp.dot(he, wup_ref[:, c0:c0 + n], preferred_element_type=F32)
        w = cw_ref[:, c0:c0 + n]
        return (u[HALO - 1:HALO - 1 + tm] * w[0:1] + u[HALO:HALO + tm] * w[1:2]
                + u[HALO + 1:HALO + 1 + tm] * w[2:3] + cb_ref[:, c0:c0 + n])

    acc = x
    c0 = 0
    for n in FFN_CHUNKS:
        act = (jax.nn.gelu(conv(c0, n)) * conv(D_FF + c0, n)).astype(BF16)
        acc = acc + jnp.dot(act, wdn_ref[c0:c0 + n, :], preferred_element_type=F32)
        c0 += n
    y_ref[0] = acc


def _ffn(x, g, w_up, conv_w, conv_b, w_down, tm):
    b, s, _ = x.shape
    nb = tm // HALO
    last = s // HALO - 1
    return pl.pallas_call(
        _ffn_kernel,
        grid=(b, s // tm),
        in_specs=[
            pl.BlockSpec((1, tm, D_MODEL), lambda i, j: (i, j, 0)),
            pl.BlockSpec((1, HALO, D_MODEL), lambda i, j: (i, jnp.maximum(j * nb - 1, 0), 0)),
            pl.BlockSpec((1, HALO, D_MODEL), lambda i, j: (i, jnp.minimum((j + 1) * nb, last), 0)),
            _const_spec((1, D_MODEL)),
            _const_spec((D_MODEL, 2 * D_FF)),
            _const_spec((3, 2 * D_FF)),
            _const_spec((1, 2 * D_FF)),
            _const_spec((D_FF, D_MODEL)),
        ],
        out_specs=pl.BlockSpec((1, tm, D_MODEL), lambda i, j: (i, j, 0)),
        out_shape=jax.ShapeDtypeStruct(x.shape, F32),
        compiler_params=_params(("parallel", "parallel")),
        name="ffn",
    )(x, x, x, g, w_up, conv_w, conv_b, w_down)


def _rope_tables(seq):
    half = HEAD_DIM // 2
    inv = jnp.power(ROPE_THETA, -jnp.arange(half, dtype=F32) * 2.0 / HEAD_DIM)
    ang = jnp.arange(seq, dtype=F32)[:, None] * inv[None, :]
    cos, sin = jnp.cos(ang), jnp.sin(ang)
    return jnp.tile(cos, (1, 4)), jnp.tile(jnp.concatenate([-sin, sin], axis=-1), (1, 2))


def _group_sum_matrix():
    idx = np.arange(256) // HEAD_DIM
    return jnp.asarray(idx[:, None] == idx[None, :], dtype=BF16)


def _tiles(seq):
    tm = min(512, seq)
    tq = min(512, seq)
    tk = min(1024, seq)
    return tm, tq, tk


def _trunk(x, weights, bias_blocks):
    seq = x.shape[1]
    tm, tq, tk = _tiles(seq)
    cos, sin = _rope_tables(seq)
    gsum = _group_sum_matrix()
    for l, w in enumerate(weights):
        lambda_init = 0.8 - 0.6 * math.exp(-0.3 * l)
        qt, k, vt, qn, kn, vn = _proj(x, w["g_mix"], w["w_att"], w["gq_a"], w["gk_a"], w["gq_n"], w["gk_n"],
                                      cos, sin, gsum, tm)
        oa = _diff_attention(qt, k, vt, w["lam_q1"], w["lam_k1"], w["lam_q2"], w["lam_k2"], w["g_sub"],
                             lambda_init, tq, tk)
        on = _na_attention(qn, kn, vn, bias_blocks[l])
        x = _merge(x, oa, on, w["g_mix"], w["w_gate"], w["w_pa"], w["w_pb"], w["w_o"], tm)
        x = _ffn(x, w["g_ffn"], w["w_up"], w["conv_w"], w["conv_b"], w["w_down"], tm)
    return x


def _layer_weights(g_mix, w_in, gq_a, gk_a, lam_q1, lam_k1, lam_q2, lam_k2, g_sub, gq_n, gk_n,
                   w_pa, w_pb, w_o, g_ffn, w_up, conv_w, conv_b, w_down):
    depth = w_in.shape[0]
    w_in16 = w_in.astype(BF16)
    w_pa16, w_pb16, w_o16 = w_pa.astype(BF16), w_pb.astype(BF16), w_o.astype(BF16)
    w_up16, w_down16 = w_up.astype(BF16), w_down.astype(BF16)
    head_tile = lambda g: jnp.tile(g.astype(F32), W_DIFF // HEAD_DIM)[None, :]
    row = lambda g: g.astype(F32)[None, :]
    out = []
    for l in range(depth):
        out.append(dict(
            g_mix=row(g_mix[l]), w_att=w_in16[l, :, :ATT_COLS], w_gate=w_in16[l, :, ATT_COLS:],
            gq_a=head_tile(gq_a[l]), gk_a=head_tile(gk_a[l]), gq_n=head_tile(gq_n[l]), gk_n=head_tile(gk_n[l]),
            lam_q1=row(lam_q1[l]), lam_k1=row(lam_k1[l]), lam_q2=row(lam_q2[l]), lam_k2=row(lam_k2[l]),
            g_sub=g_sub[l].astype(F32)[:, None],
            w_pa=w_pa16[l], w_pb=w_pb16[l], w_o=w_o16[l],
            g_ffn=row(g_ffn[l]), w_up=w_up16[l], conv_w=conv_w[l].astype(F32), conv_b=row(conv_b[l]),
            w_down=w_down16[l]))
    return out


def kernel(x_prompt, x_sample, g_mix, w_in, gq_a, gk_a, lam_q1, lam_k1, lam_q2, lam_k2, g_sub, gq_n, gk_n,
           na_bias, w_pa, w_pb, w_o, g_ffn, w_up, conv_w, conv_b, w_down):
    weights = _layer_weights(g_mix, w_in, gq_a, gk_a, lam_q1, lam_k1, lam_q2, lam_k2, g_sub, gq_n, gk_n,
                             w_pa, w_pb, w_o, g_ffn, w_up, conv_w, conv_b, w_down)
    outs = []
    for x in (x_prompt, x_sample):
        bias_blocks = _na_bias_blocks(na_bias, x.shape[1] // GRID_W)
        outs.append(_trunk(x, weights, bias_blocks))
    return tuple(outs)
```

```python
import functools
import math

import numpy as np
import jax
import jax.numpy as jnp
from jax import lax
from jax.experimental import pallas as pl
from jax.experimental.pallas import tpu as pltpu

D_MODEL = 1024
HEAD_DIM = 64
N_HEADS_DIFF = 4
N_HEADS_NA = 8
W_DIFF = N_HEADS_DIFF * 2 * HEAD_DIM
W_NA = N_HEADS_NA * HEAD_DIM
ATT_COLS = 3 * W_DIFF + 3 * W_NA
GRID_W = 64
WIN_H = 8
WIN_W = 16
D_FF = 2816
ROPE_THETA = 10000.0
EPS = 1e-6
NEG = -1e30

NA_QROWS = 4
NA_KROWS = 12
FFN_CHUNKS = (768, 768, 768, 512)
HALO = 8

VMEM_LIMIT = 56 * 1024 * 1024

F32 = jnp.float32
BF16 = jnp.bfloat16


def _const_spec(shape):
    nd = len(shape)
    return pl.BlockSpec(shape, lambda *_: (0,) * nd, pipeline_mode=pl.Buffered(1))


def _params(sem):
    return pltpu.CompilerParams(dimension_semantics=sem, vmem_limit_bytes=VMEM_LIMIT)


def _rms(x, g):
    return x * lax.rsqrt(jnp.mean(x * x, axis=-1, keepdims=True) + EPS) * g


def _proj_kernel(x_ref, gmix_ref, w_ref, gqa_ref, gka_ref, gqn_ref, gkn_ref, cos_ref, sin_ref, gsum_ref,
                 qt_ref, k_ref, vt_ref, qn_ref, kn_ref, vn_ref):
    x = x_ref[0]
    h = _rms(x, gmix_ref[...]).astype(BF16)
    gsum = gsum_ref[...]

    def proj(c0):
        return jnp.dot(h, w_ref[:, c0:c0 + W_DIFF], preferred_element_type=F32)

    def head_norm(z, g):
        z2 = z * z
        hi = z2.astype(BF16)
        lo = (z2 - hi.astype(F32)).astype(BF16)
        parts = []
        for c in range(0, W_DIFF, 256):
            parts.append(jnp.dot(hi[:, c:c + 256], gsum, preferred_element_type=F32)
                         + jnp.dot(lo[:, c:c + 256], gsum, preferred_element_type=F32))
        ss = jnp.concatenate(parts, axis=-1)
        return z * lax.rsqrt(ss * (1.0 / HEAD_DIM) + EPS) * g

    cos = jnp.concatenate([cos_ref[...]] * 4, axis=-1)
    sin = jnp.concatenate([sin_ref[...]] * 4, axis=-1)
    lane = lax.broadcasted_iota(jnp.int32, (1, W_DIFF), 1)
    first_half = (lane % HEAD_DIM) < (HEAD_DIM // 2)

    def rope(y):
        partner = jnp.where(first_half, pltpu.roll(y, W_DIFF - HEAD_DIM // 2, 1), pltpu.roll(y, HEAD_DIM // 2, 1))
        return y * cos + partner * sin

    scale = HEAD_DIM ** -0.5
    qa = rope(head_norm(proj(0), gqa_ref[...])) * (scale * LOG2E)
    qt_ref[0] = qa.T.astype(BF16)
    ka = rope(head_norm(proj(W_DIFF), gka_ref[...]))
    k_ref[0] = ka.astype(BF16)
    vt_ref[0] = proj(2 * W_DIFF).T.astype(BF16)
    qn_ref[0] = (head_norm(proj(3 * W_DIFF), gqn_ref[...]) * scale).astype(BF16)
    kn_ref[0] = head_norm(proj(3 * W_DIFF + W_NA), gkn_ref[...]).astype(BF16)
    vn_ref[0] = proj(3 * W_DIFF + 2 * W_NA).astype(BF16)


def _proj(x, gmix, w_att, gqa, gka, gqn, gkn, cos, sin, gsum, tm):
    b, s, _ = x.shape
    tok = lambda i, j: (i, j, 0)
    tr = lambda i, j: (i, 0, j)
    row_major = jax.ShapeDtypeStruct((b, s, W_DIFF), BF16)
    col_major = jax.ShapeDtypeStruct((b, W_DIFF, s), BF16)
    return pl.pallas_call(
        _proj_kernel,
        grid=(b, s // tm),
        in_specs=[
            pl.BlockSpec((1, tm, D_MODEL), tok),
            _const_spec((1, D_MODEL)),
            _const_spec((D_MODEL, ATT_COLS)),
            _const_spec((1, W_DIFF)), _const_spec((1, W_DIFF)), _const_spec((1, W_NA)), _const_spec((1, W_NA)),
            pl.BlockSpec((tm, 128), lambda i, j: (j, 0)),
            pl.BlockSpec((tm, 128), lambda i, j: (j, 0)),
            _const_spec((256, 256)),
        ],
        out_specs=[
            pl.BlockSpec((1, W_DIFF, tm), tr),
            pl.BlockSpec((1, tm, W_DIFF), tok),
            pl.BlockSpec((1, W_DIFF, tm), tr),
            pl.BlockSpec((1, tm, W_NA), tok),
            pl.BlockSpec((1, tm, W_NA), tok),
            pl.BlockSpec((1, tm, W_NA), tok),
        ],
        out_shape=[col_major, row_major, col_major, row_major, row_major, row_major],
        compiler_params=_params(("parallel", "parallel")),
        name="proj",
    )(x, gmix, w_att, gqa, gka, gqn, gkn, cos, sin, gsum)


N_MAPS = 2 * N_HEADS_DIFF
LOG2E = math.log2(math.e)
MAX_STATIC_SHIFT = 60.0


def _masked_q_group(qt_ref, j, row):
    grp = qt_ref[0, 128 * (j // 2):128 * (j // 2) + 128, :]
    keep = (row >= HEAD_DIM) if (j % 2) else (row < HEAD_DIM)
    return jnp.where(keep, grp, jnp.zeros_like(grp))


def _diff_finish(acc_sc, l_of, lq1_ref, lk1_ref, lq2_ref, lk2_ref, gsub_ref, o_ref, lambda_init):
    lam = (jnp.exp(jnp.sum(lq1_ref[...] * lk1_ref[...])) - jnp.exp(jnp.sum(lq2_ref[...] * lk2_ref[...]))
           + lambda_init)
    for h in range(N_HEADS_DIFF):
        o0 = acc_sc[h] / l_of(h)
        o1 = acc_sc[N_HEADS_DIFF + h] / l_of(N_HEADS_DIFF + h)
        o = o0 - lam * o1
        o = o * lax.rsqrt(jnp.mean(o * o, axis=0, keepdims=True) + EPS) * gsub_ref[...] * (1.0 - lambda_init)
        o_ref[0, :, 128 * h:128 * h + 128] = o.T.astype(o_ref.dtype)


def _diff_online_kernel(shift_ref, qt_ref, k_ref, vt_ref, lq1_ref, lk1_ref, lq2_ref, lk2_ref, gsub_ref, o_ref,
                        qz_sc, m_sc, l_sc, acc_sc, *, lambda_init):
    del shift_ref
    kv = pl.program_id(2)
    tq = qt_ref.shape[2]

    @pl.when(kv == 0)
    def _init():
        m_sc[...] = jnp.full(m_sc.shape, NEG, F32)
        l_sc[...] = jnp.zeros(l_sc.shape, F32)
        acc_sc[...] = jnp.zeros(acc_sc.shape, F32)
        row = lax.broadcasted_iota(jnp.int32, (128, tq), 0)
        for j in range(N_MAPS):
            qz_sc[j] = _masked_q_group(qt_ref, j, row)

    for j in range(N_MAPS):
        h = j % N_HEADS_DIFF
        kp = k_ref[0, :, 128 * (j // 2):128 * (j // 2) + 128]
        s = jnp.dot(kp, qz_sc[j], preferred_element_type=F32)
        m_prev = m_sc[j]
        m_cur = jnp.maximum(m_prev, jnp.max(s, axis=0, keepdims=True))
        alpha = jnp.exp2(m_prev - m_cur)
        p = jnp.exp2(s - m_cur)
        l_sc[j] = alpha * l_sc[j] + jnp.sum(p, axis=0, keepdims=True)
        vt = vt_ref[0, 128 * h:128 * h + 128, :]
        acc_sc[j] = alpha * acc_sc[j] + jnp.dot(vt, p.astype(BF16), preferred_element_type=F32)
        m_sc[j] = m_cur

    @pl.when(kv == pl.num_programs(2) - 1)
    def _finish():
        _diff_finish(acc_sc, lambda j: l_sc[j], lq1_ref, lk1_ref, lq2_ref, lk2_ref, gsub_ref, o_ref, lambda_init)


def _diff_static_kernel(shift_ref, qt_ref, k_ref, vt_ref, lq1_ref, lk1_ref, lq2_ref, lk2_ref, gsub_ref, o_ref,
                        qz_sc, l_sc, acc_sc, *, lambda_init):
    kv = pl.program_id(2)
    tq = qt_ref.shape[2]
    tk = k_ref.shape[1]

    @pl.when(kv == 0)
    def _init():
        l_sc[...] = jnp.zeros(l_sc.shape, F32)
        acc_sc[...] = jnp.zeros(acc_sc.shape, F32)
        row = lax.broadcasted_iota(jnp.int32, (128, tq), 0)
        shift_rows = jnp.where(row == 0, -shift_ref[0], 0.0).astype(BF16)
        for j in range(N_MAPS):
            qz_sc[j, :128, :] = _masked_q_group(qt_ref, j, row)
            qz_sc[j, 128:, :] = shift_rows

    ones = jnp.ones((tk, 128), BF16)
    for j in range(N_MAPS):
        h = j % N_HEADS_DIFF
        kaug = jnp.concatenate([k_ref[0, :, 128 * (j // 2):128 * (j // 2) + 128], ones], axis=1)
        p = jnp.exp2(jnp.dot(kaug, qz_sc[j], preferred_element_type=F32))
        l_sc[j] += jnp.sum(p.reshape(tk // 8, 8, tq), axis=0)
        vt = vt_ref[0, 128 * h:128 * h + 128, :]
        acc_sc[j] += jnp.dot(vt, p.astype(BF16), preferred_element_type=F32)

    @pl.when(kv == pl.num_programs(2) - 1)
    def _finish():
        _diff_finish(acc_sc, lambda j: jnp.sum(l_sc[j], axis=0, keepdims=True),
                     lq1_ref, lk1_ref, lq2_ref, lk2_ref, gsub_ref, o_ref, lambda_init)


def _diff_call(body, scratch, operands, lambda_init, tq, tk):
    qt = operands[1]
    b, _, s = qt.shape
    return pl.pallas_call(
        functools.partial(body, lambda_init=lambda_init),
        grid=(b, s // tq, s // tk),
        in_specs=[
            pl.BlockSpec(memory_space=pltpu.SMEM),
            pl.BlockSpec((1, W_DIFF, tq), lambda i, j, t: (i, 0, j)),
            pl.BlockSpec((1, tk, W_DIFF), lambda i, j, t: (i, t, 0)),
            pl.BlockSpec((1, W_DIFF, tk), lambda i, j, t: (i, 0, t)),
            _const_spec((1, HEAD_DIM)), _const_spec((1, HEAD_DIM)), _const_spec((1, HEAD_DIM)),
            _const_spec((1, HEAD_DIM)),
            _const_spec((2 * HEAD_DIM, 1)),
        ],
        out_specs=pl.BlockSpec((1, tq, W_DIFF), lambda i, j, t: (i, j, 0)),
        out_shape=jax.ShapeDtypeStruct((b, s, W_DIFF), BF16),
        scratch_shapes=scratch,
        compiler_params=_params(("parallel", "parallel", "arbitrary")),
        name=body.__name__.strip("_"),
    )(*operands)


def _diff_attention(shift, qt, k, vt, lq1, lk1, lq2, lk2, gsub, lambda_init, tq, tk):
    operands = (shift, qt, k, vt, lq1, lk1, lq2, lk2, gsub)
    static = functools.partial(
        _diff_call, _diff_static_kernel,
        [pltpu.VMEM((N_MAPS, 256, tq), BF16), pltpu.VMEM((N_MAPS, 8, tq), F32), pltpu.VMEM((N_MAPS, 128, tq), F32)],
        lambda_init=lambda_init, tq=tq, tk=tk)
    online = functools.partial(
        _diff_call, _diff_online_kernel,
        [pltpu.VMEM((N_MAPS, 128, tq), BF16), pltpu.VMEM((N_MAPS, 1, tq), F32), pltpu.VMEM((N_MAPS, 1, tq), F32),
         pltpu.VMEM((N_MAPS, 128, tq), F32)],
        lambda_init=lambda_init, tq=tq, tk=tk)
    return lax.cond(shift[0] <= MAX_STATIC_SHIFT, static, online, operands)


def _na_selectors(rows):
    kh = min(WIN_H, rows)
    assert kh == WIN_H and rows >= NA_KROWS + NA_QROWS
    rsel = np.zeros((3, NA_QROWS, NA_KROWS, 2 * WIN_H - 1), np.float32)
    for v in range(3):
        r0 = (0, NA_QROWS, rows - NA_QROWS)[v]
        ws = min(max(r0 - 4, 0), rows - NA_KROWS)
        for t in range(NA_QROWS):
            r = r0 + t
            rs = min(max(r - kh // 2, 0), rows - kh)
            for i in range(NA_KROWS):
                kr = ws + i
                if rs <= kr < rs + kh:
                    rsel[v, t, i, kr - r + WIN_H - 1] = 1.0
    csel = np.zeros((GRID_W, GRID_W, 2 * WIN_W - 1), np.float32)
    for c in range(GRID_W):
        cs = min(max(c - WIN_W // 2, 0), GRID_W - WIN_W)
        for j in range(cs, cs + WIN_W):
            csel[c, j, j - c + WIN_W - 1] = 1.0
    return rsel, csel


def _na_bias_blocks(na_bias, rows):
    rsel, csel = _na_selectors(rows)
    dense = jnp.einsum("lhde,vtid,cje->lvhtcij", na_bias.astype(F32), rsel, csel,
                       precision=lax.Precision.HIGHEST)
    valid = np.einsum("vtid,cje->vtcij", rsel, csel) > 0.5
    dense = jnp.where(valid[None, :, None], dense, NEG)
    depth = na_bias.shape[0]
    return dense.reshape(depth, 3, N_HEADS_NA, NA_QROWS * GRID_W, NA_KROWS * GRID_W)


def _na_kernel(q_ref, k0_ref, k1_ref, k2_ref, v0_ref, v1_ref, v2_ref, bias_ref, o_ref):
    q = q_ref[0]
    kwin = jnp.concatenate([k0_ref[0], k1_ref[0], k2_ref[0]], axis=0)
    vwin = jnp.concatenate([v0_ref[0], v1_ref[0], v2_ref[0]], axis=0)
    outs = []
    for h in range(N_HEADS_NA):
        sl = slice(HEAD_DIM * h, HEAD_DIM * (h + 1))
        s = lax.dot_general(q[:, sl], kwin[:, sl], (((1,), (1,)), ((), ())), preferred_element_type=F32)
        s = s + bias_ref[0, h]
        p = jnp.exp(s - jnp.max(s, axis=-1, keepdims=True))
        l = jnp.sum(p, axis=-1, keepdims=True)
        outs.append(jnp.dot(p.astype(BF16), vwin[:, sl], preferred_element_type=F32) / l)
    o_ref[0] = jnp.concatenate(outs, axis=-1).astype(o_ref.dtype)


def _na_attention(qn, kn, vn, bias_blocks):
    b, s, _ = qn.shape
    rows = s // GRID_W
    nblk = rows // NA_QROWS
    tq = NA_QROWS * GRID_W

    def win(d):
        return lambda i, j: (i, jnp.clip(j - 1, 0, nblk - 3) + d, 0)

    variant = lambda i, j: (jnp.where(j == 0, 0, jnp.where(j == nblk - 1, 2, 1)), 0, 0, 0)
    blk = lambda f: pl.BlockSpec((1, tq, W_NA), f)
    return pl.pallas_call(
        _na_kernel,
        grid=(b, nblk),
        in_specs=[blk(lambda i, j: (i, j, 0)),
                  blk(win(0)), blk(win(1)), blk(win(2)),
                  blk(win(0)), blk(win(1)), blk(win(2)),
                  pl.BlockSpec((1, N_HEADS_NA, tq, NA_KROWS * GRID_W), variant)],
        out_specs=blk(lambda i, j: (i, j, 0)),
        out_shape=jax.ShapeDtypeStruct((b, s, W_NA), BF16),
        compiler_params=_params(("parallel", "arbitrary")),
        name="natt",
    )(qn, kn, kn, kn, vn, vn, vn, bias_blocks)


def _merge_kernel(x_ref, oa_ref, on_ref, gmix_ref, wg_ref, wpa_ref, wpb_ref, wo_ref, y_ref):
    x = x_ref[0]
    h = _rms(x, gmix_ref[...]).astype(BF16)
    ga = jnp.dot(h, wg_ref[:, :D_MODEL], preferred_element_type=F32)
    gb = jnp.dot(h, wg_ref[:, D_MODEL:], preferred_element_type=F32)
    pa = jnp.dot(oa_ref[0], wpa_ref[...], preferred_element_type=F32)
    pb = jnp.dot(on_ref[0], wpb_ref[...], preferred_element_type=F32)
    mixed = jax.nn.sigmoid(ga) * pa + jax.nn.sigmoid(gb) * pb
    y_ref[0] = x + jnp.dot(mixed.astype(BF16), wo_ref[...], preferred_element_type=F32)


def _merge(x, oa, on, gmix, w_gate, w_pa, w_pb, w_o, tm):
    b, s, _ = x.shape
    tok = lambda i, j: (i, j, 0)
    return pl.pallas_call(
        _merge_kernel,
        grid=(b, s // tm),
        in_specs=[
            pl.BlockSpec((1, tm, D_MODEL), tok),
            pl.BlockSpec((1, tm, W_DIFF), tok),
            pl.BlockSpec((1, tm, W_NA), tok),
            _const_spec((1, D_MODEL)),
            _const_spec((D_MODEL, 2 * D_MODEL)),
            _const_spec((W_DIFF, D_MODEL)),
            _const_spec((W_NA, D_MODEL)),
            _const_spec((D_MODEL, D_MODEL)),
        ],
        out_specs=pl.BlockSpec((1, tm, D_MODEL), tok),
        out_shape=jax.ShapeDtypeStruct(x.shape, F32),
        compiler_params=_params(("parallel", "parallel")),
        name="merge",
    )(x, oa, on, gmix, w_gate, w_pa, w_pb, w_o)


def _ffn_kernel(x_ref, xp_ref, xn_ref, g_ref, wup_ref, cw_ref, cb_ref, wdn_ref, y_ref):
    j = pl.program_id(1)
    tm = x_ref.shape[1]
    x = x_ref[0]
    xe = jnp.concatenate([xp_ref[0], x, xn_ref[0]], axis=0)
    row = lax.broadcasted_iota(jnp.int32, (tm + 2 * HALO, 1), 0)
    inside = jnp.logical_and(jnp.logical_or(row >= HALO, j > 0),
                             jnp.logical_or(row < tm + HALO, j < pl.num_programs(1) - 1))
    he = jnp.where(inside, _rms(xe, g_ref[...]), 0.0).astype(BF16)

    def conv(c0, n):
        u = jnp.dot(he, wup_ref[:, c0:c0 + n], preferred_element_type=F32)
        w = cw_ref[:, c0:c0 + n]
        return (u[HALO - 1:HALO - 1 + tm] * w[0:1] + u[HALO:HALO + tm] * w[1:2]
                + u[HALO + 1:HALO + 1 + tm] * w[2:3] + cb_ref[:, c0:c0 + n])

    acc = x
    c0 = 0
    for n in FFN_CHUNKS:
        act = (jax.nn.gelu(conv(c0, n)) * conv(D_FF + c0, n)).astype(BF16)
        acc = acc + jnp.dot(act, wdn_ref[c0:c0 + n, :], preferred_element_type=F32)
        c0 += n
    y_ref[0] = acc


def _ffn(x, g, w_up, conv_w, conv_b, w_down, tm):
    b, s, _ = x.shape
    nb = tm // HALO
    last = s // HALO - 1
    return pl.pallas_call(
        _ffn_kernel,
        grid=(b, s // tm),
        in_specs=[
            pl.BlockSpec((1, tm, D_MODEL), lambda i, j: (i, j, 0)),
            pl.BlockSpec((1, HALO, D_MODEL), lambda i, j: (i, jnp.maximum(j * nb - 1, 0), 0)),
            pl.BlockSpec((1, HALO, D_MODEL), lambda i, j: (i, jnp.minimum((j + 1) * nb, last), 0)),
            _const_spec((1, D_MODEL)),
            _const_spec((D_MODEL, 2 * D_FF)),
            _const_spec((3, 2 * D_FF)),
            _const_spec((1, 2 * D_FF)),
            _const_spec((D_FF, D_MODEL)),
        ],
        out_specs=pl.BlockSpec((1, tm, D_MODEL), lambda i, j: (i, j, 0)),
        out_shape=jax.ShapeDtypeStruct(x.shape, F32),
        compiler_params=_params(("parallel", "parallel")),
        name="ffn",
    )(x, x, x, g, w_up, conv_w, conv_b, w_down)


def _rope_tables(seq):
    half = HEAD_DIM // 2
    inv = jnp.power(ROPE_THETA, -jnp.arange(half, dtype=F32) * 2.0 / HEAD_DIM)
    ang = jnp.arange(seq, dtype=F32)[:, None] * inv[None, :]
    cos, sin = jnp.cos(ang), jnp.sin(ang)
    return jnp.tile(cos, (1, 4)), jnp.tile(jnp.concatenate([-sin, sin], axis=-1), (1, 2))


def _group_sum_matrix():
    idx = np.arange(256) // HEAD_DIM
    return jnp.asarray(idx[:, None] == idx[None, :], dtype=BF16)


def _tiles(seq):
    tm = min(512, seq)
    tq = min(512, seq)
    tk = min(1024, seq)
    return tm, tq, tk


def _trunk(x, weights, bias_blocks):
    seq = x.shape[1]
    tm, tq, tk = _tiles(seq)
    cos, sin = _rope_tables(seq)
    gsum = _group_sum_matrix()
    for l, w in enumerate(weights):
        lambda_init = 0.8 - 0.6 * math.exp(-0.3 * l)
        qt, k, vt, qn, kn, vn = _proj(x, w["g_mix"], w["w_att"], w["gq_a"], w["gk_a"], w["gq_n"], w["gk_n"],
                                      cos, sin, gsum, tm)
        oa = _diff_attention(w["shift"], qt, k, vt, w["lam_q1"], w["lam_k1"], w["lam_q2"], w["lam_k2"], w["g_sub"],
                             lambda_init, tq, tk)
        on = _na_attention(qn, kn, vn, bias_blocks[l])
        x = _merge(x, oa, on, w["g_mix"], w["w_gate"], w["w_pa"], w["w_pb"], w["w_o"], tm)
        x = _ffn(x, w["g_ffn"], w["w_up"], w["conv_w"], w["conv_b"], w["w_down"], tm)
    return x


def _score_bound(gq, gk):
    bound = (HEAD_DIM ** 0.5) * LOG2E * jnp.max(jnp.abs(gq)) * jnp.max(jnp.abs(gk))
    return jnp.ceil(1.02 * bound.astype(F32)).reshape(1)


def _layer_weights(g_mix, w_in, gq_a, gk_a, lam_q1, lam_k1, lam_q2, lam_k2, g_sub, gq_n, gk_n,
                   w_pa, w_pb, w_o, g_ffn, w_up, conv_w, conv_b, w_down):
    depth = w_in.shape[0]
    w_in16 = w_in.astype(BF16)
    w_pa16, w_pb16, w_o16 = w_pa.astype(BF16), w_pb.astype(BF16), w_o.astype(BF16)
    w_up16, w_down16 = w_up.astype(BF16), w_down.astype(BF16)
    head_tile = lambda g: jnp.tile(g.astype(F32), W_DIFF // HEAD_DIM)[None, :]
    row = lambda g: g.astype(F32)[None, :]
    out = []
    for l in range(depth):
        out.append(dict(
            g_mix=row(g_mix[l]), w_att=w_in16[l, :, :ATT_COLS], w_gate=w_in16[l, :, ATT_COLS:],
            gq_a=head_tile(gq_a[l]), gk_a=head_tile(gk_a[l]), gq_n=head_tile(gq_n[l]), gk_n=head_tile(gk_n[l]),
            lam_q1=row(lam_q1[l]), lam_k1=row(lam_k1[l]), lam_q2=row(lam_q2[l]), lam_k2=row(lam_k2[l]),
            g_sub=g_sub[l].astype(F32)[:, None], shift=_score_bound(gq_a[l], gk_a[l]),
            w_pa=w_pa16[l], w_pb=w_pb16[l], w_o=w_o16[l],
            g_ffn=row(g_ffn[l]), w_up=w_up16[l], conv_w=conv_w[l].astype(F32), conv_b=row(conv_b[l]),
            w_down=w_down16[l]))
    return out


def kernel(x_prompt, x_sample, g_mix, w_in, gq_a, gk_a, lam_q1, lam_k1, lam_q2, lam_k2, g_sub, gq_n, gk_n,
           na_bias, w_pa, w_pb, w_o, g_ffn, w_up, conv_w, conv_b, w_down):
    weights = _layer_weights(g_mix, w_in, gq_a, gk_a, lam_q1, lam_k1, lam_q2, lam_k2, g_sub, gq_n, gk_n,
                             w_pa, w_pb, w_o, g_ffn, w_up, conv_w, conv_b, w_down)
    outs = []
    for x in (x_prompt, x_sample):
        bias_blocks = _na_bias_blocks(na_bias, x.shape[1] // GRID_W)
        outs.append(_trunk(x, weights, bias_blocks))
    return tuple(outs)
```

```python
import functools
import math

import numpy as np
import jax
import jax.numpy as jnp
from jax import lax
from jax.experimental import pallas as pl
from jax.experimental.pallas import tpu as pltpu

D_MODEL = 1024
HEAD_DIM = 64
N_HEADS_DIFF = 4
N_HEADS_NA = 8
W_DIFF = N_HEADS_DIFF * 2 * HEAD_DIM
W_NA = N_HEADS_NA * HEAD_DIM
ATT_COLS = 3 * W_DIFF + 3 * W_NA
GRID_W = 64
WIN_H = 8
WIN_W = 16
D_FF = 2816
ROPE_THETA = 10000.0
EPS = 1e-6
NEG = -1e30

NA_QROWS = 4
NA_KROWS = 12
FFN_CHUNKS = (768, 768, 768, 512)
HALO = 8

VMEM_LIMIT = 56 * 1024 * 1024

F32 = jnp.float32
BF16 = jnp.bfloat16


def _const_spec(shape):
    nd = len(shape)
    return pl.BlockSpec(shape, lambda *_: (0,) * nd, pipeline_mode=pl.Buffered(1))


def _params(sem):
    return pltpu.CompilerParams(dimension_semantics=sem, vmem_limit_bytes=VMEM_LIMIT)


def _rms(x, g):
    return x * lax.rsqrt(jnp.mean(x * x, axis=-1, keepdims=True) + EPS) * g


def _proj_kernel(x_ref, gmix_ref, w_ref, gqa_ref, gka_ref, gqn_ref, gkn_ref, cos_ref, sin_ref, gsum_ref,
                 qt_ref, k_ref, vt_ref, qnt_ref, kn_ref, vnt_ref):
    x = x_ref[0]
    h = _rms(x, gmix_ref[...]).astype(BF16)
    gsum = gsum_ref[...]

    def proj(c0):
        return jnp.dot(h, w_ref[:, c0:c0 + W_DIFF], preferred_element_type=F32)

    def head_norm(z, g):
        z2 = z * z
        hi = z2.astype(BF16)
        lo = (z2 - hi.astype(F32)).astype(BF16)
        parts = []
        for c in range(0, W_DIFF, 256):
            parts.append(jnp.dot(hi[:, c:c + 256], gsum, preferred_element_type=F32)
                         + jnp.dot(lo[:, c:c + 256], gsum, preferred_element_type=F32))
        ss = jnp.concatenate(parts, axis=-1)
        return z * lax.rsqrt(ss * (1.0 / HEAD_DIM) + EPS) * g

    cos = jnp.concatenate([cos_ref[...]] * 4, axis=-1)
    sin = jnp.concatenate([sin_ref[...]] * 4, axis=-1)
    lane = lax.broadcasted_iota(jnp.int32, (1, W_DIFF), 1)
    first_half = (lane % HEAD_DIM) < (HEAD_DIM // 2)

    def rope(y):
        partner = jnp.where(first_half, pltpu.roll(y, W_DIFF - HEAD_DIM // 2, 1), pltpu.roll(y, HEAD_DIM // 2, 1))
        return y * cos + partner * sin

    scale = HEAD_DIM ** -0.5 * LOG2E
    qa = rope(head_norm(proj(0), gqa_ref[...])) * scale
    qt_ref[0] = qa.T.astype(BF16)
    ka = rope(head_norm(proj(W_DIFF), gka_ref[...]))
    k_ref[0] = ka.astype(BF16)
    vt_ref[0] = proj(2 * W_DIFF).T.astype(BF16)
    qnt_ref[0] = (head_norm(proj(3 * W_DIFF), gqn_ref[...]) * scale).T.astype(BF16)
    kn_ref[0] = head_norm(proj(3 * W_DIFF + W_NA), gkn_ref[...]).astype(BF16)
    vnt_ref[0] = proj(3 * W_DIFF + 2 * W_NA).T.astype(BF16)


def _proj(x, gmix, w_att, gqa, gka, gqn, gkn, cos, sin, gsum, tm):
    b, s, _ = x.shape
    tok = lambda i, j: (i, j, 0)
    tr = lambda i, j: (i, 0, j)
    row_major = jax.ShapeDtypeStruct((b, s, W_DIFF), BF16)
    col_major = jax.ShapeDtypeStruct((b, W_DIFF, s), BF16)
    return pl.pallas_call(
        _proj_kernel,
        grid=(b, s // tm),
        in_specs=[
            pl.BlockSpec((1, tm, D_MODEL), tok),
            _const_spec((1, D_MODEL)),
            _const_spec((D_MODEL, ATT_COLS)),
            _const_spec((1, W_DIFF)), _const_spec((1, W_DIFF)), _const_spec((1, W_NA)), _const_spec((1, W_NA)),
            pl.BlockSpec((tm, 128), lambda i, j: (j, 0)),
            pl.BlockSpec((tm, 128), lambda i, j: (j, 0)),
            _const_spec((256, 256)),
        ],
        out_specs=[
            pl.BlockSpec((1, W_DIFF, tm), tr),
            pl.BlockSpec((1, tm, W_DIFF), tok),
            pl.BlockSpec((1, W_DIFF, tm), tr),
            pl.BlockSpec((1, W_NA, tm), tr),
            pl.BlockSpec((1, tm, W_NA), tok),
            pl.BlockSpec((1, W_NA, tm), tr),
        ],
        out_shape=[col_major, row_major, col_major, col_major, row_major, col_major],
        compiler_params=_params(("parallel", "parallel")),
        name="proj",
    )(x, gmix, w_att, gqa, gka, gqn, gkn, cos, sin, gsum)


N_MAPS = 2 * N_HEADS_DIFF
LOG2E = math.log2(math.e)
MAX_STATIC_SHIFT = 60.0


def _masked_q_group(qt_ref, j, row):
    grp = qt_ref[0, 128 * (j // 2):128 * (j // 2) + 128, :]
    keep = (row >= HEAD_DIM) if (j % 2) else (row < HEAD_DIM)
    return jnp.where(keep, grp, jnp.zeros_like(grp))


def _diff_finish(acc_sc, l_of, lq1_ref, lk1_ref, lq2_ref, lk2_ref, gsub_ref, o_ref, lambda_init):
    lam = (jnp.exp(jnp.sum(lq1_ref[...] * lk1_ref[...])) - jnp.exp(jnp.sum(lq2_ref[...] * lk2_ref[...]))
           + lambda_init)
    for h in range(N_HEADS_DIFF):
        o0 = acc_sc[h] / l_of(h)
        o1 = acc_sc[N_HEADS_DIFF + h] / l_of(N_HEADS_DIFF + h)
        o = o0 - lam * o1
        o = o * lax.rsqrt(jnp.mean(o * o, axis=0, keepdims=True) + EPS) * gsub_ref[...] * (1.0 - lambda_init)
        o_ref[0, :, 128 * h:128 * h + 128] = o.T.astype(o_ref.dtype)


def _diff_online_kernel(shift_ref, qt_ref, k_ref, vt_ref, lq1_ref, lk1_ref, lq2_ref, lk2_ref, gsub_ref, o_ref,
                        qz_sc, m_sc, l_sc, acc_sc, *, lambda_init):
    del shift_ref
    kv = pl.program_id(2)
    tq = qt_ref.shape[2]

    @pl.when(kv == 0)
    def _init():
        m_sc[...] = jnp.full(m_sc.shape, NEG, F32)
        l_sc[...] = jnp.zeros(l_sc.shape, F32)
        acc_sc[...] = jnp.zeros(acc_sc.shape, F32)
        row = lax.broadcasted_iota(jnp.int32, (128, tq), 0)
        for j in range(N_MAPS):
            qz_sc[j] = _masked_q_group(qt_ref, j, row)

    for j in range(N_MAPS):
        h = j % N_HEADS_DIFF
        kp = k_ref[0, :, 128 * (j // 2):128 * (j // 2) + 128]
        s = jnp.dot(kp, qz_sc[j], preferred_element_type=F32)
        m_prev = m_sc[j]
        m_cur = jnp.maximum(m_prev, jnp.max(s, axis=0, keepdims=True))
        alpha = jnp.exp2(m_prev - m_cur)
        p = jnp.exp2(s - m_cur)
        l_sc[j] = alpha * l_sc[j] + jnp.sum(p, axis=0, keepdims=True)
        vt = vt_ref[0, 128 * h:128 * h + 128, :]
        acc_sc[j] = alpha * acc_sc[j] + jnp.dot(vt, p.astype(BF16), preferred_element_type=F32)
        m_sc[j] = m_cur

    @pl.when(kv == pl.num_programs(2) - 1)
    def _finish():
        _diff_finish(acc_sc, lambda j: l_sc[j], lq1_ref, lk1_ref, lq2_ref, lk2_ref, gsub_ref, o_ref, lambda_init)


def _diff_static_kernel(shift_ref, qt_ref, k_ref, vt_ref, lq1_ref, lk1_ref, lq2_ref, lk2_ref, gsub_ref, o_ref,
                        qz_sc, l_sc, acc_sc, *, lambda_init):
    kv = pl.program_id(2)
    tq = qt_ref.shape[2]
    tk = k_ref.shape[1]

    @pl.when(kv == 0)
    def _init():
        l_sc[...] = jnp.zeros(l_sc.shape, F32)
        acc_sc[...] = jnp.zeros(acc_sc.shape, F32)
        row = lax.broadcasted_iota(jnp.int32, (128, tq), 0)
        shift_rows = jnp.where(row == 0, -shift_ref[0], 0.0).astype(BF16)
        for j in range(N_MAPS):
            qz_sc[j, :128, :] = _masked_q_group(qt_ref, j, row)
            qz_sc[j, 128:, :] = shift_rows

    ones = jnp.ones((tk, 128), BF16)

    def scores(j):
        kaug = jnp.concatenate([k_ref[0, :, 128 * (j // 2):128 * (j // 2) + 128], ones], axis=1)
        return jnp.dot(kaug, qz_sc[j], preferred_element_type=F32)

    s_next = scores(0)
    for j in range(N_MAPS):
        h = j % N_HEADS_DIFF
        s = s_next
        if j + 1 < N_MAPS:
            s_next = scores(j + 1)
        p = jnp.exp2(s)
        l_sc[j] += jnp.sum(p.reshape(tk // 8, 8, tq), axis=0)
        vt = vt_ref[0, 128 * h:128 * h + 128, :]
        acc_sc[j] += jnp.dot(vt, p.astype(BF16), preferred_element_type=F32)

    @pl.when(kv == pl.num_programs(2) - 1)
    def _finish():
        _diff_finish(acc_sc, lambda j: jnp.sum(l_sc[j], axis=0, keepdims=True),
                     lq1_ref, lk1_ref, lq2_ref, lk2_ref, gsub_ref, o_ref, lambda_init)


def _diff_call(body, scratch, operands, lambda_init, tq, tk):
    qt = operands[1]
    b, _, s = qt.shape
    return pl.pallas_call(
        functools.partial(body, lambda_init=lambda_init),
        grid=(b, s // tq, s // tk),
        in_specs=[
            pl.BlockSpec(memory_space=pltpu.SMEM),
            pl.BlockSpec((1, W_DIFF, tq), lambda i, j, t: (i, 0, j)),
            pl.BlockSpec((1, tk, W_DIFF), lambda i, j, t: (i, t, 0)),
            pl.BlockSpec((1, W_DIFF, tk), lambda i, j, t: (i, 0, t)),
            _const_spec((1, HEAD_DIM)), _const_spec((1, HEAD_DIM)), _const_spec((1, HEAD_DIM)),
            _const_spec((1, HEAD_DIM)),
            _const_spec((2 * HEAD_DIM, 1)),
        ],
        out_specs=pl.BlockSpec((1, tq, W_DIFF), lambda i, j, t: (i, j, 0)),
        out_shape=jax.ShapeDtypeStruct((b, s, W_DIFF), BF16),
        scratch_shapes=scratch,
        compiler_params=_params(("parallel", "parallel", "arbitrary")),
        name=body.__name__.strip("_"),
    )(*operands)


def _diff_attention(shift, qt, k, vt, lq1, lk1, lq2, lk2, gsub, lambda_init, tq, tk):
    operands = (shift, qt, k, vt, lq1, lk1, lq2, lk2, gsub)
    static = functools.partial(
        _diff_call, _diff_static_kernel,
        [pltpu.VMEM((N_MAPS, 256, tq), BF16), pltpu.VMEM((N_MAPS, 8, tq), F32), pltpu.VMEM((N_MAPS, 128, tq), F32)],
        lambda_init=lambda_init, tq=tq, tk=tk)
    online = functools.partial(
        _diff_call, _diff_online_kernel,
        [pltpu.VMEM((N_MAPS, 128, tq), BF16), pltpu.VMEM((N_MAPS, 1, tq), F32), pltpu.VMEM((N_MAPS, 1, tq), F32),
         pltpu.VMEM((N_MAPS, 128, tq), F32)],
        lambda_init=lambda_init, tq=tq, tk=tk)
    return lax.cond(shift[0] <= MAX_STATIC_SHIFT, static, online, operands)


def _na_selectors(rows):
    kh = min(WIN_H, rows)
    assert kh == WIN_H and rows >= NA_KROWS + NA_QROWS
    rsel = np.zeros((3, NA_QROWS, NA_KROWS, 2 * WIN_H - 1), np.float32)
    for v in range(3):
        r0 = (0, NA_QROWS, rows - NA_QROWS)[v]
        ws = min(max(r0 - 4, 0), rows - NA_KROWS)
        for t in range(NA_QROWS):
            r = r0 + t
            rs = min(max(r - kh // 2, 0), rows - kh)
            for i in range(NA_KROWS):
                kr = ws + i
                if rs <= kr < rs + kh:
                    rsel[v, t, i, kr - r + WIN_H - 1] = 1.0
    csel = np.zeros((GRID_W, GRID_W, 2 * WIN_W - 1), np.float32)
    for c in range(GRID_W):
        cs = min(max(c - WIN_W // 2, 0), GRID_W - WIN_W)
        for j in range(cs, cs + WIN_W):
            csel[c, j, j - c + WIN_W - 1] = 1.0
    return rsel, csel


def _na_bias_blocks(na_bias, rows, shifts):
    rsel, csel = _na_selectors(rows)
    dense = jnp.einsum("lhde,vtid,cje->lvhijtc", na_bias.astype(F32), rsel, csel,
                       precision=lax.Precision.HIGHEST)
    dense = dense * LOG2E - shifts.reshape((-1,) + (1,) * 6)
    valid = np.einsum("vtid,cje->vijtc", rsel, csel) > 0.5
    dense = jnp.where(valid[None, :, None], dense, NEG)
    depth = na_bias.shape[0]
    return dense.reshape(depth, 3, N_HEADS_NA, NA_KROWS * GRID_W, NA_QROWS * GRID_W)


def _na_kernel(qt_ref, k0_ref, k1_ref, k2_ref, v0_ref, v1_ref, v2_ref, bias_ref, o_ref, *, subtract_max):
    tq = qt_ref.shape[2]
    kwin = jnp.concatenate([k0_ref[0], k1_ref[0], k2_ref[0]], axis=0)
    vtwin = jnp.concatenate([v0_ref[0], v1_ref[0], v2_ref[0]], axis=1)
    row = lax.broadcasted_iota(jnp.int32, (128, tq), 0)

    def scores(h):
        return jnp.dot(kwin[:, 128 * (h // 2):128 * (h // 2) + 128], _masked_q_group(qt_ref, h, row),
                       preferred_element_type=F32) + bias_ref[0, h]

    outs = []
    s_next = scores(0)
    for h in range(N_HEADS_NA):
        s = s_next
        if h + 1 < N_HEADS_NA:
            s_next = scores(h + 1)
        if subtract_max:
            s = s - jnp.max(s, axis=0, keepdims=True)
        p = jnp.exp2(s)
        l = jnp.sum(p, axis=0, keepdims=True)
        vt = vtwin[HEAD_DIM * h:HEAD_DIM * (h + 1), :]
        outs.append(jnp.dot(vt, p.astype(BF16), preferred_element_type=F32) / l)
    o_ref[0] = jnp.concatenate(outs, axis=0).T.astype(o_ref.dtype)


def _na_call(subtract_max, operands):
    qnt, kn, vnt, bias_blocks = operands
    b, s, _ = kn.shape
    rows = s // GRID_W
    nblk = rows // NA_QROWS
    tq = NA_QROWS * GRID_W
    first = lambda j: jnp.clip(j - 1, 0, nblk - 3)
    variant = lambda i, j: (jnp.where(j == 0, 0, jnp.where(j == nblk - 1, 2, 1)), 0, 0, 0)
    kblk = lambda d: pl.BlockSpec((1, tq, W_NA), lambda i, j: (i, first(j) + d, 0))
    vblk = lambda d: pl.BlockSpec((1, W_NA, tq), lambda i, j: (i, 0, first(j) + d))
    return pl.pallas_call(
        functools.partial(_na_kernel, subtract_max=subtract_max),
        grid=(b, nblk),
        in_specs=[pl.BlockSpec((1, W_NA, tq), lambda i, j: (i, 0, j)),
                  kblk(0), kblk(1), kblk(2), vblk(0), vblk(1), vblk(2),
                  pl.BlockSpec((1, N_HEADS_NA, NA_KROWS * GRID_W, tq), variant)],
        out_specs=pl.BlockSpec((1, tq, W_NA), lambda i, j: (i, j, 0)),
        out_shape=jax.ShapeDtypeStruct((b, s, W_NA), BF16),
        compiler_params=_params(("parallel", "arbitrary")),
        name="natt_online" if subtract_max else "natt_static",
    )(qnt, kn, kn, kn, vnt, vnt, vnt, bias_blocks)


def _na_attention(shift_is_static, qnt, kn, vnt, bias_blocks):
    return lax.cond(shift_is_static, functools.partial(_na_call, False), functools.partial(_na_call, True),
                    (qnt, kn, vnt, bias_blocks))


def _merge_kernel(x_ref, oa_ref, on_ref, gmix_ref, wg_ref, wpa_ref, wpb_ref, wo_ref, y_ref):
    x = x_ref[0]
    h = _rms(x, gmix_ref[...]).astype(BF16)
    ga = jnp.dot(h, wg_ref[:, :D_MODEL], preferred_element_type=F32)
    gb = jnp.dot(h, wg_ref[:, D_MODEL:], preferred_element_type=F32)
    pa = jnp.dot(oa_ref[0], wpa_ref[...], preferred_element_type=F32)
    pb = jnp.dot(on_ref[0], wpb_ref[...], preferred_element_type=F32)
    mixed = jax.nn.sigmoid(ga) * pa + jax.nn.sigmoid(gb) * pb
    y_ref[0] = x + jnp.dot(mixed.astype(BF16), wo_ref[...], preferred_element_type=F32)


def _merge(x, oa, on, gmix, w_gate, w_pa, w_pb, w_o, tm):
    b, s, _ = x.shape
    tok = lambda i, j: (i, j, 0)
    return pl.pallas_call(
        _merge_kernel,
        grid=(b, s // tm),
        in_specs=[
            pl.BlockSpec((1, tm, D_MODEL), tok),
            pl.BlockSpec((1, tm, W_DIFF), tok),
            pl.BlockSpec((1, tm, W_NA), tok),
            _const_spec((1, D_MODEL)),
            _const_spec((D_MODEL, 2 * D_MODEL)),
            _const_spec((W_DIFF, D_MODEL)),
            _const_spec((W_NA, D_MODEL)),
            _const_spec((D_MODEL, D_MODEL)),
        ],
        out_specs=pl.BlockSpec((1, tm, D_MODEL), tok),
        out_shape=jax.ShapeDtypeStruct(x.shape, F32),
        compiler_params=_params(("parallel", "parallel")),
        name="merge",
    )(x, oa, on, gmix, w_gate, w_pa, w_pb, w_o)


def _ffn_kernel(x_ref, xp_ref, xn_ref, g_ref, wup_ref, cw_ref, cb_ref, wdn_ref, y_ref):
    j = pl.program_id(1)
    tm = x_ref.shape[1]
    x = x_ref[0]
    xe = jnp.concatenate([xp_ref[0], x, xn_ref[0]], axis=0)
    row = lax.broadcasted_iota(jnp.int32, (tm + 2 * HALO, 1), 0)
    inside = jnp.logical_and(jnp.logical_or(row >= HALO, j > 0),
                             jnp.logical_or(row < tm + HALO, j < pl.num_programs(1) - 1))
    he = jnp.where(inside, _rms(xe, g_ref[...]), 0.0).astype(BF16)

    def up(c0, n):
        return tuple(jnp.dot(he, wup_ref[:, c:c + n], preferred_element_type=F32)
                     for c in (c0, D_FF + c0))

    def conv(u, c0, n):
        w = cw_ref[:, c0:c0 + n]
        return (u[HALO - 1:HALO - 1 + tm] * w[0:1] + u[HALO:HALO + tm] * w[1:2]
                + u[HALO + 1:HALO + 1 + tm] * w[2:3] + cb_ref[:, c0:c0 + n])

    starts = [sum(FFN_CHUNKS[:i]) for i in range(len(FFN_CHUNKS))]
    acc = x
    u_next = up(starts[0], FFN_CHUNKS[0])
    for i, (c0, n) in enumerate(zip(starts, FFN_CHUNKS)):
        ug, uv = u_next
        if i + 1 < len(FFN_CHUNKS):
            u_next = up(starts[i + 1], FFN_CHUNKS[i + 1])
        act = (jax.nn.gelu(conv(ug, c0, n)) * conv(uv, D_FF + c0, n)).astype(BF16)
        acc = acc + jnp.dot(act, wdn_ref[c0:c0 + n, :], preferred_element_type=F32)
    y_ref[0] = acc


def _ffn(x, g, w_up, conv_w, conv_b, w_down, tm):
    b, s, _ = x.shape
    nb = tm // HALO
    last = s // HALO - 1
    return pl.pallas_call(
        _ffn_kernel,
        grid=(b, s // tm),
        in_specs=[
            pl.BlockSpec((1, tm, D_MODEL), lambda i, j: (i, j, 0)),
            pl.BlockSpec((1, HALO, D_MODEL), lambda i, j: (i, jnp.maximum(j * nb - 1, 0), 0)),
            pl.BlockSpec((1, HALO, D_MODEL), lambda i, j: (i, jnp.minimum((j + 1) * nb, last), 0)),
            _const_spec((1, D_MODEL)),
            _const_spec((D_MODEL, 2 * D_FF)),
            _const_spec((3, 2 * D_FF)),
            _const_spec((1, 2 * D_FF)),
            _const_spec((D_FF, D_MODEL)),
        ],
        out_specs=pl.BlockSpec((1, tm, D_MODEL), lambda i, j: (i, j, 0)),
        out_shape=jax.ShapeDtypeStruct(x.shape, F32),
        compiler_params=_params(("parallel", "parallel")),
        name="ffn",
    )(x, x, x, g, w_up, conv_w, conv_b, w_down)


def _rope_tables(seq):
    half = HEAD_DIM // 2
    inv = jnp.power(ROPE_THETA, -jnp.arange(half, dtype=F32) * 2.0 / HEAD_DIM)
    ang = jnp.arange(seq, dtype=F32)[:, None] * inv[None, :]
    cos, sin = jnp.cos(ang), jnp.sin(ang)
    return jnp.tile(cos, (1, 4)), jnp.tile(jnp.concatenate([-sin, sin], axis=-1), (1, 2))


def _group_sum_matrix():
    idx = np.arange(256) // HEAD_DIM
    return jnp.asarray(idx[:, None] == idx[None, :], dtype=BF16)


def _tiles(seq):
    tm = min(512, seq)
    tq = min(1024, seq)
    tk = min(1024, seq)
    return tm, tq, tk


def _trunk(x, weights, na_bias, na_shift):
    seq = x.shape[1]
    tm, tq, tk = _tiles(seq)
    cos, sin = _rope_tables(seq)
    gsum = _group_sum_matrix()
    na_static = na_shift <= MAX_STATIC_SHIFT
    bias_blocks = _na_bias_blocks(na_bias, seq // GRID_W, jnp.where(na_static, na_shift, 0.0))
    for l, w in enumerate(weights):
        lambda_init = 0.8 - 0.6 * math.exp(-0.3 * l)
        qt, k, vt, qnt, kn, vnt = _proj(x, w["g_mix"], w["w_att"], w["gq_a"], w["gk_a"], w["gq_n"], w["gk_n"],
                                        cos, sin, gsum, tm)
        oa = _diff_attention(w["shift"], qt, k, vt, w["lam_q1"], w["lam_k1"], w["lam_q2"], w["lam_k2"], w["g_sub"],
                             lambda_init, tq, tk)
        on = _na_attention(na_static[l], qnt, kn, vnt, bias_blocks[l])
        x = _merge(x, oa, on, w["g_mix"], w["w_gate"], w["w_pa"], w["w_pb"], w["w_o"], tm)
        x = _ffn(x, w["g_ffn"], w["w_up"], w["conv_w"], w["conv_b"], w["w_down"], tm)
    return x


def _score_bound(gq, gk):
    bound = (HEAD_DIM ** 0.5) * LOG2E * jnp.max(jnp.abs(gq)) * jnp.max(jnp.abs(gk))
    return jnp.ceil(1.02 * bound.astype(F32)).reshape(1)


def _na_score_bound(gq, gk, na_bias):
    amax = lambda a: jnp.max(jnp.abs(a.astype(F32)).reshape(a.shape[0], -1), axis=1)
    bound = LOG2E * ((HEAD_DIM ** 0.5) * amax(gq) * amax(gk) + amax(na_bias))
    return jnp.ceil(1.02 * bound)


def _layer_weights(g_mix, w_in, gq_a, gk_a, lam_q1, lam_k1, lam_q2, lam_k2, g_sub, gq_n, gk_n,
                   w_pa, w_pb, w_o, g_ffn, w_up, conv_w, conv_b, w_down):
    depth = w_in.shape[0]
    w_in16 = w_in.astype(BF16)
    w_pa16, w_pb16, w_o16 = w_pa.astype(BF16), w_pb.astype(BF16), w_o.astype(BF16)
    w_up16, w_down16 = w_up.astype(BF16), w_down.astype(BF16)
    head_tile = lambda g: jnp.tile(g.astype(F32), W_DIFF // HEAD_DIM)[None, :]
    row = lambda g: g.astype(F32)[None, :]
    out = []
    for l in range(depth):
        out.append(dict(
            g_mix=row(g_mix[l]), w_att=w_in16[l, :, :ATT_COLS], w_gate=w_in16[l, :, ATT_COLS:],
            gq_a=head_tile(gq_a[l]), gk_a=head_tile(gk_a[l]), gq_n=head_tile(gq_n[l]), gk_n=head_tile(gk_n[l]),
            lam_q1=row(lam_q1[l]), lam_k1=row(lam_k1[l]), lam_q2=row(lam_q2[l]), lam_k2=row(lam_k2[l]),
            g_sub=g_sub[l].astype(F32)[:, None], shift=_score_bound(gq_a[l], gk_a[l]),
            w_pa=w_pa16[l], w_pb=w_pb16[l], w_o=w_o16[l],
            g_ffn=row(g_ffn[l]), w_up=w_up16[l], conv_w=conv_w[l].astype(F32), conv_b=row(conv_b[l]),
            w_down=w_down16[l]))
    return out


def kernel(x_prompt, x_sample, g_mix, w_in, gq_a, gk_a, lam_q1, lam_k1, lam_q2, lam_k2, g_sub, gq_n, gk_n,
           na_bias, w_pa, w_pb, w_o, g_ffn, w_up, conv_w, conv_b, w_down):
    weights = _layer_weights(g_mix, w_in, gq_a, gk_a, lam_q1, lam_k1, lam_q2, lam_k2, g_sub, gq_n, gk_n,
                             w_pa, w_pb, w_o, g_ffn, w_up, conv_w, conv_b, w_down)
    na_shift = _na_score_bound(gq_n, gk_n, na_bias)
    return tuple(_trunk(x, weights, na_bias, na_shift) for x in (x_prompt, x_sample))
```

```python
import functools
import math

import numpy as np
import jax
import jax.numpy as jnp
from jax import lax
from jax.experimental import pallas as pl
from jax.experimental.pallas import tpu as pltpu

D_MODEL = 1024
HEAD_DIM = 64
N_HEADS_DIFF = 4
N_HEADS_NA = 8
W_DIFF = N_HEADS_DIFF * 2 * HEAD_DIM
W_NA = N_HEADS_NA * HEAD_DIM
ATT_COLS = 3 * W_DIFF + 3 * W_NA
GRID_W = 64
WIN_H = 8
WIN_W = 16
D_FF = 2816
ROPE_THETA = 10000.0
EPS = 1e-6
NEG = -1e30

NA_QROWS = 4
NA_KROWS = 12
FFN_CHUNKS = (768, 768, 768, 512)
HALO = 8

VMEM_LIMIT = 56 * 1024 * 1024

F32 = jnp.float32
BF16 = jnp.bfloat16


def _const_spec(shape, index=None):
    index = (0,) * len(shape) if index is None else tuple(index)
    return pl.BlockSpec(shape, lambda *_: index, pipeline_mode=pl.Buffered(1))


def _params(sem):
    return pltpu.CompilerParams(dimension_semantics=sem, vmem_limit_bytes=VMEM_LIMIT)


def _rms(x, g):
    return x * lax.rsqrt(jnp.mean(x * x, axis=-1, keepdims=True) + EPS) * g


def _proj_kernel(x_ref, gmix_ref, w_ref, gqa_ref, gka_ref, gqn_ref, gkn_ref, cos_ref, sin_ref, cost_ref, sint_ref,
                 gsum_ref, qt_ref, k_ref, vt_ref, qnt_ref, kn_ref, vnt_ref):
    x = x_ref[0]
    h = _rms(x, gmix_ref[...]).astype(BF16)
    gsum = gsum_ref[...]
    half = HEAD_DIM // 2

    def proj(c0):
        return jnp.dot(h, w_ref[:, c0:c0 + W_DIFF], preferred_element_type=F32)

    def head_norm_t(zt, gcol):
        heads = []
        for r in range(0, W_DIFF, HEAD_DIM):
            blk = zt[r:r + HEAD_DIM]
            heads.append(blk * lax.rsqrt(jnp.mean(blk * blk, axis=0, keepdims=True) + EPS) * gcol)
        return heads

    def rope_t(blk):
        x1, x2 = blk[:half], blk[half:]
        return jnp.concatenate([x1 * cost_ref[...] - x2 * sint_ref[...], x1 * sint_ref[...] + x2 * cost_ref[...]],
                               axis=0)

    def head_norm(z, g):
        z2 = z * z
        hi = z2.astype(BF16)
        lo = (z2 - hi.astype(F32)).astype(BF16)
        parts = []
        for c in range(0, W_DIFF, 256):
            parts.append(jnp.dot(hi[:, c:c + 256], gsum, preferred_element_type=F32)
                         + jnp.dot(lo[:, c:c + 256], gsum, preferred_element_type=F32))
        ss = jnp.concatenate(parts, axis=-1)
        return z * lax.rsqrt(ss * (1.0 / HEAD_DIM) + EPS) * g

    cos = jnp.concatenate([cos_ref[...]] * 4, axis=-1)
    sin = jnp.concatenate([sin_ref[...]] * 4, axis=-1)
    lane = lax.broadcasted_iota(jnp.int32, (1, W_DIFF), 1)
    first_half = (lane % HEAD_DIM) < (HEAD_DIM // 2)

    def rope(y):
        partner = jnp.where(first_half, pltpu.roll(y, W_DIFF - HEAD_DIM // 2, 1), pltpu.roll(y, HEAD_DIM // 2, 1))
        return y * cos + partner * sin

    scale = HEAD_DIM ** -0.5 * LOG2E

    def emit_qa(z):
        qa = [rope_t(blk) for blk in head_norm_t(z.T, gqa_ref[...])]
        qt_ref[0] = (jnp.concatenate(qa, axis=0) * scale).astype(BF16)

    def emit_ka(z):
        k_ref[0] = rope(head_norm(z, gka_ref[...])).astype(BF16)

    def emit_va(z):
        vt_ref[0] = z.T.astype(BF16)

    def emit_qn(z):
        qnt_ref[0] = (jnp.concatenate(head_norm_t(z.T, gqn_ref[...]), axis=0) * scale).astype(BF16)

    def emit_kn(z):
        kn_ref[0] = head_norm(z, gkn_ref[...]).astype(BF16)

    def emit_vn(z):
        vnt_ref[0] = z.T.astype(BF16)

    emitters = (emit_qa, emit_ka, emit_va, emit_qn, emit_kn, emit_vn)
    z_next = proj(0)
    for i, emit in enumerate(emitters):
        z = z_next
        if i + 1 < len(emitters):
            z_next = proj((i + 1) * W_DIFF)
        emit(z)


def _proj(x, layer, gmix, w_in, gqa, gka, gqn, gkn, cos, sin, cost, sint, gsum, tm):
    b, s, _ = x.shape
    tok = lambda i, j: (i, j, 0)
    tr = lambda i, j: (i, 0, j)
    row_major = jax.ShapeDtypeStruct((b, s, W_DIFF), BF16)
    col_major = jax.ShapeDtypeStruct((b, W_DIFF, s), BF16)
    half = HEAD_DIM // 2
    return pl.pallas_call(
        _proj_kernel,
        grid=(b, s // tm),
        in_specs=[
            pl.BlockSpec((1, tm, D_MODEL), tok),
            _const_spec((1, D_MODEL)),
            _const_spec((None, D_MODEL, ATT_COLS), (layer, 0, 0)),
            _const_spec((HEAD_DIM, 1)), _const_spec((1, W_DIFF)), _const_spec((HEAD_DIM, 1)), _const_spec((1, W_NA)),
            pl.BlockSpec((tm, 128), lambda i, j: (j, 0)),
            pl.BlockSpec((tm, 128), lambda i, j: (j, 0)),
            pl.BlockSpec((half, tm), lambda i, j: (0, j)),
            pl.BlockSpec((half, tm), lambda i, j: (0, j)),
            _const_spec((256, 256)),
        ],
        out_specs=[
            pl.BlockSpec((1, W_DIFF, tm), tr),
            pl.BlockSpec((1, tm, W_DIFF), tok),
            pl.BlockSpec((1, W_DIFF, tm), tr),
            pl.BlockSpec((1, W_NA, tm), tr),
            pl.BlockSpec((1, tm, W_NA), tok),
            pl.BlockSpec((1, W_NA, tm), tr),
        ],
        out_shape=[col_major, row_major, col_major, col_major, row_major, col_major],
        compiler_params=_params(("parallel", "parallel")),
        name="proj",
    )(x, gmix, w_in, gqa, gka, gqn, gkn, cos, sin, cost, sint, gsum)


N_MAPS = 2 * N_HEADS_DIFF
LOG2E = math.log2(math.e)
MAX_STATIC_SHIFT = 60.0


def _masked_q_group(qt_ref, j, row):
    grp = qt_ref[0, 128 * (j // 2):128 * (j // 2) + 128, :]
    keep = (row >= HEAD_DIM) if (j % 2) else (row < HEAD_DIM)
    return jnp.where(keep, grp, jnp.zeros_like(grp))


def _diff_finish(acc_sc, l_of, lq1_ref, lk1_ref, lq2_ref, lk2_ref, gsub_ref, o_ref, lambda_init):
    lam = (jnp.exp(jnp.sum(lq1_ref[...] * lk1_ref[...])) - jnp.exp(jnp.sum(lq2_ref[...] * lk2_ref[...]))
           + lambda_init)
    for h in range(N_HEADS_DIFF):
        o0 = acc_sc[h] / l_of(h)
        o1 = acc_sc[N_HEADS_DIFF + h] / l_of(N_HEADS_DIFF + h)
        o = o0 - lam * o1
        o = o * lax.rsqrt(jnp.mean(o * o, axis=0, keepdims=True) + EPS) * gsub_ref[...] * (1.0 - lambda_init)
        o_ref[0, :, 128 * h:128 * h + 128] = o.T.astype(o_ref.dtype)


def _diff_online_kernel(shift_ref, qt_ref, k_ref, vt_ref, lq1_ref, lk1_ref, lq2_ref, lk2_ref, gsub_ref, o_ref,
                        qz_sc, m_sc, l_sc, acc_sc, *, lambda_init):
    del shift_ref
    kv = pl.program_id(2)
    tq = qt_ref.shape[2]

    @pl.when(kv == 0)
    def _init():
        m_sc[...] = jnp.full(m_sc.shape, NEG, F32)
        l_sc[...] = jnp.zeros(l_sc.shape, F32)
        acc_sc[...] = jnp.zeros(acc_sc.shape, F32)
        row = lax.broadcasted_iota(jnp.int32, (128, tq), 0)
        for j in range(N_MAPS):
            qz_sc[j] = _masked_q_group(qt_ref, j, row)

    for j in range(N_MAPS):
        h = j % N_HEADS_DIFF
        kp = k_ref[0, :, 128 * (j // 2):128 * (j // 2) + 128]
        s = jnp.dot(kp, qz_sc[j], preferred_element_type=F32)
        m_prev = m_sc[j]
        m_cur = jnp.maximum(m_prev, jnp.max(s, axis=0, keepdims=True))
        alpha = jnp.exp2(m_prev - m_cur)
        p = jnp.exp2(s - m_cur)
        l_sc[j] = alpha * l_sc[j] + jnp.sum(p, axis=0, keepdims=True)
        vt = vt_ref[0, 128 * h:128 * h + 128, :]
        acc_sc[j] = alpha * acc_sc[j] + jnp.dot(vt, p.astype(BF16), preferred_element_type=F32)
        m_sc[j] = m_cur

    @pl.when(kv == pl.num_programs(2) - 1)
    def _finish():
        _diff_finish(acc_sc, lambda j: l_sc[j], lq1_ref, lk1_ref, lq2_ref, lk2_ref, gsub_ref, o_ref, lambda_init)


def _diff_static_kernel(shift_ref, qt_ref, k_ref, vt_ref, lq1_ref, lk1_ref, lq2_ref, lk2_ref, gsub_ref, o_ref,
                        qz_sc, l_sc, acc_sc, *, lambda_init):
    kv = pl.program_id(2)
    tq = qt_ref.shape[2]
    tk = k_ref.shape[1]

    @pl.when(kv == 0)
    def _init():
        l_sc[...] = jnp.zeros(l_sc.shape, F32)
        acc_sc[...] = jnp.zeros(acc_sc.shape, F32)
        row = lax.broadcasted_iota(jnp.int32, (128, tq), 0)
        shift_rows = jnp.where(row == 0, -shift_ref[0], 0.0).astype(BF16)
        for j in range(N_MAPS):
            qz_sc[j, :128, :] = _masked_q_group(qt_ref, j, row)
            qz_sc[j, 128:, :] = shift_rows

    ones = jnp.ones((tk, 128), BF16)

    def scores(j):
        kaug = jnp.concatenate([k_ref[0, :, 128 * (j // 2):128 * (j // 2) + 128], ones], axis=1)
        return jnp.dot(kaug, qz_sc[j], preferred_element_type=F32)

    s_next = scores(0)
    for j in range(N_MAPS):
        h = j % N_HEADS_DIFF
        s = s_next
        if j + 1 < N_MAPS:
            s_next = scores(j + 1)
        p = jnp.exp2(s)
        l_sc[j] += jnp.sum(p.reshape(tk // 8, 8, tq), axis=0)
        vt = vt_ref[0, 128 * h:128 * h + 128, :]
        acc_sc[j] += jnp.dot(vt, p.astype(BF16), preferred_element_type=F32)

    @pl.when(kv == pl.num_programs(2) - 1)
    def _finish():
        _diff_finish(acc_sc, lambda j: jnp.sum(l_sc[j], axis=0, keepdims=True),
                     lq1_ref, lk1_ref, lq2_ref, lk2_ref, gsub_ref, o_ref, lambda_init)


def _diff_call(body, scratch, operands, lambda_init, tq, tk):
    qt = operands[1]
    b, _, s = qt.shape
    return pl.pallas_call(
        functools.partial(body, lambda_init=lambda_init),
        grid=(b, s // tq, s // tk),
        in_specs=[
            pl.BlockSpec(memory_space=pltpu.SMEM),
            pl.BlockSpec((1, W_DIFF, tq), lambda i, j, t: (i, 0, j)),
            pl.BlockSpec((1, tk, W_DIFF), lambda i, j, t: (i, t, 0)),
            pl.BlockSpec((1, W_DIFF, tk), lambda i, j, t: (i, 0, t)),
            _const_spec((1, HEAD_DIM)), _const_spec((1, HEAD_DIM)), _const_spec((1, HEAD_DIM)),
            _const_spec((1, HEAD_DIM)),
            _const_spec((2 * HEAD_DIM, 1)),
        ],
        out_specs=pl.BlockSpec((1, tq, W_DIFF), lambda i, j, t: (i, j, 0)),
        out_shape=jax.ShapeDtypeStruct((b, s, W_DIFF), BF16),
        scratch_shapes=scratch,
        compiler_params=_params(("parallel", "parallel", "arbitrary")),
        name=body.__name__.strip("_"),
    )(*operands)


def _diff_attention(shift, qt, k, vt, lq1, lk1, lq2, lk2, gsub, lambda_init, tq, tk):
    operands = (shift, qt, k, vt, lq1, lk1, lq2, lk2, gsub)
    static = functools.partial(
        _diff_call, _diff_static_kernel,
        [pltpu.VMEM((N_MAPS, 256, tq), BF16), pltpu.VMEM((N_MAPS, 8, tq), F32), pltpu.VMEM((N_MAPS, 128, tq), F32)],
        lambda_init=lambda_init, tq=tq, tk=tk)
    online = functools.partial(
        _diff_call, _diff_online_kernel,
        [pltpu.VMEM((N_MAPS, 128, tq), BF16), pltpu.VMEM((N_MAPS, 1, tq), F32), pltpu.VMEM((N_MAPS, 1, tq), F32),
         pltpu.VMEM((N_MAPS, 128, tq), F32)],
        lambda_init=lambda_init, tq=tq, tk=tk)
    return lax.cond(shift[0] <= MAX_STATIC_SHIFT, static, online, operands)


def _na_selectors(rows):
    kh = min(WIN_H, rows)
    assert kh == WIN_H and rows >= NA_KROWS + NA_QROWS
    rsel = np.zeros((3, NA_QROWS, NA_KROWS, 2 * WIN_H - 1), np.float32)
    for v in range(3):
        r0 = (0, NA_QROWS, rows - NA_QROWS)[v]
        ws = min(max(r0 - 4, 0), rows - NA_KROWS)
        for t in range(NA_QROWS):
            r = r0 + t
            rs = min(max(r - kh // 2, 0), rows - kh)
            for i in range(NA_KROWS):
                kr = ws + i
                if rs <= kr < rs + kh:
                    rsel[v, t, i, kr - r + WIN_H - 1] = 1.0
    csel = np.zeros((GRID_W, GRID_W, 2 * WIN_W - 1), np.float32)
    for c in range(GRID_W):
        cs = min(max(c - WIN_W // 2, 0), GRID_W - WIN_W)
        for j in range(cs, cs + WIN_W):
            csel[c, j, j - c + WIN_W - 1] = 1.0
    return rsel, csel


def _na_bias_blocks(na_bias, rows, shifts):
    rsel, csel = _na_selectors(rows)
    slab = jnp.einsum("lhde,cje->lhdjc", na_bias.astype(F32), csel, precision=lax.Precision.HIGHEST)
    slab = slab * LOG2E - shifts.reshape(-1, 1, 1, 1, 1)
    slab = jnp.where((csel.sum(-1) > 0.5).T, slab, NEG)
    slab = jnp.concatenate([slab, jnp.full_like(slab[:, :, :1], NEG)], axis=2)
    rel_row = np.where(rsel.sum(-1) > 0.5, rsel.argmax(-1), 2 * WIN_H - 1)
    variants = []
    for v in range(3):
        key_rows = [jnp.concatenate([slab[:, :, rel_row[v, t, i]] for t in range(NA_QROWS)], axis=-1)
                    for i in range(NA_KROWS)]
        variants.append(jnp.concatenate(key_rows, axis=2))
    return jnp.stack(variants, axis=1)


def _na_kernel(qt_ref, k0_ref, k1_ref, k2_ref, v0_ref, v1_ref, v2_ref, bias_ref, o_ref, *, subtract_max):
    tq = qt_ref.shape[2]
    kwin = jnp.concatenate([k0_ref[0], k1_ref[0], k2_ref[0]], axis=0)
    vtwin = jnp.concatenate([v0_ref[0], v1_ref[0], v2_ref[0]], axis=1)
    row = lax.broadcasted_iota(jnp.int32, (128, tq), 0)

    def scores(h):
        return jnp.dot(kwin[:, 128 * (h // 2):128 * (h // 2) + 128], _masked_q_group(qt_ref, h, row),
                       preferred_element_type=F32) + bias_ref[0, h]

    outs = []
    s_next = scores(0)
    for h in range(N_HEADS_NA):
        s = s_next
        if h + 1 < N_HEADS_NA:
            s_next = scores(h + 1)
        if subtract_max:
            s = s - jnp.max(s, axis=0, keepdims=True)
        p = jnp.exp2(s)
        l = jnp.sum(p, axis=0, keepdims=True)
        vt = vtwin[HEAD_DIM * h:HEAD_DIM * (h + 1), :]
        outs.append(jnp.dot(vt, p.astype(BF16), preferred_element_type=F32) / l)
    o_ref[0] = jnp.concatenate(outs, axis=0).T.astype(o_ref.dtype)


def _na_call(subtract_max, operands):
    qnt, kn, vnt, bias_blocks = operands
    b, s, _ = kn.shape
    rows = s // GRID_W
    nblk = rows // NA_QROWS
    tq = NA_QROWS * GRID_W
    first = lambda j: jnp.clip(j - 1, 0, nblk - 3)
    variant = lambda i, j: (jnp.where(j == 0, 0, jnp.where(j == nblk - 1, 2, 1)), 0, 0, 0)
    kblk = lambda d: pl.BlockSpec((1, tq, W_NA), lambda i, j: (i, first(j) + d, 0))
    vblk = lambda d: pl.BlockSpec((1, W_NA, tq), lambda i, j: (i, 0, first(j) + d))
    return pl.pallas_call(
        functools.partial(_na_kernel, subtract_max=subtract_max),
        grid=(b, nblk),
        in_specs=[pl.BlockSpec((1, W_NA, tq), lambda i, j: (i, 0, j)),
                  kblk(0), kblk(1), kblk(2), vblk(0), vblk(1), vblk(2),
                  pl.BlockSpec((1, N_HEADS_NA, NA_KROWS * GRID_W, tq), variant)],
        out_specs=pl.BlockSpec((1, tq, W_NA), lambda i, j: (i, j, 0)),
        out_shape=jax.ShapeDtypeStruct((b, s, W_NA), BF16),
        compiler_params=_params(("parallel", "arbitrary")),
        name="natt_online" if subtract_max else "natt_static",
    )(qnt, kn, kn, kn, vnt, vnt, vnt, bias_blocks)


def _na_attention(shift_is_static, qnt, kn, vnt, bias_blocks):
    return lax.cond(shift_is_static, functools.partial(_na_call, False), functools.partial(_na_call, True),
                    (qnt, kn, vnt, bias_blocks))


def _merge_kernel(x_ref, oa_ref, on_ref, gmix_ref, wga_ref, wgb_ref, wpa_ref, wpb_ref, wo_ref, y_ref):
    x = x_ref[0]
    h = _rms(x, gmix_ref[...]).astype(BF16)
    ga = jnp.dot(h, wga_ref[...], preferred_element_type=F32)
    gb = jnp.dot(h, wgb_ref[...], preferred_element_type=F32)
    pa = jnp.dot(oa_ref[0], wpa_ref[...], preferred_element_type=F32)
    pb = jnp.dot(on_ref[0], wpb_ref[...], preferred_element_type=F32)
    mixed = jax.nn.sigmoid(ga) * pa + jax.nn.sigmoid(gb) * pb
    y_ref[0] = x + jnp.dot(mixed.astype(BF16), wo_ref[...], preferred_element_type=F32)


def _merge(x, oa, on, layer, gmix, w_in, w_pa, w_pb, w_o, tm):
    b, s, _ = x.shape
    tok = lambda i, j: (i, j, 0)
    gate_block = ATT_COLS // D_MODEL
    return pl.pallas_call(
        _merge_kernel,
        grid=(b, s // tm),
        in_specs=[
            pl.BlockSpec((1, tm, D_MODEL), tok),
            pl.BlockSpec((1, tm, W_DIFF), tok),
            pl.BlockSpec((1, tm, W_NA), tok),
            _const_spec((1, D_MODEL)),
            _const_spec((None, D_MODEL, D_MODEL), (layer, 0, gate_block)),
            _const_spec((None, D_MODEL, D_MODEL), (layer, 0, gate_block + 1)),
            _const_spec((None, W_DIFF, D_MODEL), (layer, 0, 0)),
            _const_spec((None, W_NA, D_MODEL), (layer, 0, 0)),
            _const_spec((None, D_MODEL, D_MODEL), (layer, 0, 0)),
        ],
        out_specs=pl.BlockSpec((1, tm, D_MODEL), tok),
        out_shape=jax.ShapeDtypeStruct(x.shape, F32),
        compiler_params=_params(("parallel", "parallel")),
        name="merge",
    )(x, oa, on, gmix, w_in, w_in, w_pa, w_pb, w_o)


def _ffn_kernel(x_ref, xp_ref, xn_ref, g_ref, wup_ref, cw_ref, cb_ref, wdn_ref, y_ref):
    j = pl.program_id(1)
    tm = x_ref.shape[1]
    x = x_ref[0]
    xe = jnp.concatenate([xp_ref[0], x, xn_ref[0]], axis=0)
    row = lax.broadcasted_iota(jnp.int32, (tm + 2 * HALO, 1), 0)
    inside = jnp.logical_and(jnp.logical_or(row >= HALO, j > 0),
                             jnp.logical_or(row < tm + HALO, j < pl.num_programs(1) - 1))
    he = jnp.where(inside, _rms(xe, g_ref[...]), 0.0).astype(BF16)

    def up(c0, n):
        return tuple(jnp.dot(he, wup_ref[:, c:c + n], preferred_element_type=F32)
                     for c in (c0, D_FF + c0))

    def conv(u, c0, n):
        w = cw_ref[:, c0:c0 + n]
        return (u[HALO - 1:HALO - 1 + tm] * w[0:1] + u[HALO:HALO + tm] * w[1:2]
                + u[HALO + 1:HALO + 1 + tm] * w[2:3] + cb_ref[:, c0:c0 + n])

    starts = [sum(FFN_CHUNKS[:i]) for i in range(len(FFN_CHUNKS))]
    acc = x
    u_next = up(starts[0], FFN_CHUNKS[0])
    for i, (c0, n) in enumerate(zip(starts, FFN_CHUNKS)):
        ug, uv = u_next
        if i + 1 < len(FFN_CHUNKS):
            u_next = up(starts[i + 1], FFN_CHUNKS[i + 1])
        act = (jax.nn.gelu(conv(ug, c0, n)) * conv(uv, D_FF + c0, n)).astype(BF16)
        acc = acc + jnp.dot(act, wdn_ref[c0:c0 + n, :], preferred_element_type=F32)
    y_ref[0] = acc


def _ffn(x, layer, g, w_up, conv_w, conv_b, w_down, tm):
    b, s, _ = x.shape
    nb = tm // HALO
    last = s // HALO - 1
    return pl.pallas_call(
        _ffn_kernel,
        grid=(b, s // tm),
        in_specs=[
            pl.BlockSpec((1, tm, D_MODEL), lambda i, j: (i, j, 0)),
            pl.BlockSpec((1, HALO, D_MODEL), lambda i, j: (i, jnp.maximum(j * nb - 1, 0), 0)),
            pl.BlockSpec((1, HALO, D_MODEL), lambda i, j: (i, jnp.minimum((j + 1) * nb, last), 0)),
            _const_spec((1, D_MODEL)),
            _const_spec((None, D_MODEL, 2 * D_FF), (layer, 0, 0)),
            _const_spec((None, 3, 2 * D_FF), (layer, 0, 0)),
            _const_spec((None, 1, 2 * D_FF), (layer, 0, 0)),
            _const_spec((None, D_FF, D_MODEL), (layer, 0, 0)),
        ],
        out_specs=pl.BlockSpec((1, tm, D_MODEL), lambda i, j: (i, j, 0)),
        out_shape=jax.ShapeDtypeStruct(x.shape, F32),
        compiler_params=_params(("parallel", "parallel")),
        name="ffn",
    )(x, x, x, g, w_up, conv_w, conv_b, w_down)


def _rope_tables(seq):
    half = HEAD_DIM // 2
    inv = jnp.power(ROPE_THETA, -jnp.arange(half, dtype=F32) * 2.0 / HEAD_DIM)
    ang = jnp.arange(seq, dtype=F32)[:, None] * inv[None, :]
    cos, sin = jnp.cos(ang), jnp.sin(ang)
    return jnp.tile(cos, (1, 4)), jnp.tile(jnp.concatenate([-sin, sin], axis=-1), (1, 2)), cos.T, sin.T


def _group_sum_matrix():
    idx = np.arange(256) // HEAD_DIM
    return jnp.asarray(idx[:, None] == idx[None, :], dtype=BF16)


def _tiles(seq):
    tm = min(512, seq)
    tq = min(1024, seq)
    tk = min(1024, seq)
    return tm, tq, tk


def _trunk(x, weights, na_bias, na_shift):
    stacked, per_layer = weights
    seq = x.shape[1]
    tm, tq, tk = _tiles(seq)
    cos, sin, cost, sint = _rope_tables(seq)
    gsum = _group_sum_matrix()
    na_static = na_shift <= MAX_STATIC_SHIFT
    bias_blocks = _na_bias_blocks(na_bias, seq // GRID_W, jnp.where(na_static, na_shift, 0.0))
    for l, w in enumerate(per_layer):
        lambda_init = 0.8 - 0.6 * math.exp(-0.3 * l)
        qt, k, vt, qnt, kn, vnt = _proj(x, l, w["g_mix"], stacked["w_in"], w["gq_a"], w["gk_a"], w["gq_n"],
                                        w["gk_n"], cos, sin, cost, sint, gsum, tm)
        oa = _diff_attention(w["shift"], qt, k, vt, w["lam_q1"], w["lam_k1"], w["lam_q2"], w["lam_k2"], w["g_sub"],
                             lambda_init, tq, tk)
        on = _na_attention(na_static[l], qnt, kn, vnt, bias_blocks[l])
        x = _merge(x, oa, on, l, w["g_mix"], stacked["w_in"], stacked["w_pa"], stacked["w_pb"], stacked["w_o"], tm)
        x = _ffn(x, l, w["g_ffn"], stacked["w_up"], stacked["conv_w"], stacked["conv_b"], stacked["w_down"], tm)
    return x


def _score_bound(gq, gk):
    bound = (HEAD_DIM ** 0.5) * LOG2E * jnp.max(jnp.abs(gq)) * jnp.max(jnp.abs(gk))
    return jnp.ceil(1.02 * bound.astype(F32)).reshape(1)


def _na_score_bound(gq, gk, na_bias):
    amax = lambda a: jnp.max(jnp.abs(a.astype(F32)).reshape(a.shape[0], -1), axis=1)
    bound = LOG2E * ((HEAD_DIM ** 0.5) * amax(gq) * amax(gk) + amax(na_bias))
    return jnp.ceil(1.02 * bound)


def _layer_weights(g_mix, w_in, gq_a, gk_a, lam_q1, lam_k1, lam_q2, lam_k2, g_sub, gq_n, gk_n,
                   w_pa, w_pb, w_o, g_ffn, w_up, conv_w, conv_b, w_down):
    depth = w_in.shape[0]
    stacked = dict(w_in=w_in.astype(BF16), w_pa=w_pa.astype(BF16), w_pb=w_pb.astype(BF16), w_o=w_o.astype(BF16),
                   w_up=w_up.astype(BF16), w_down=w_down.astype(BF16), conv_w=conv_w.astype(F32),
                   conv_b=conv_b.astype(F32)[:, None, :])
    head_tile = lambda g: jnp.tile(g.astype(F32), W_DIFF // HEAD_DIM)[None, :]
    row = lambda g: g.astype(F32)[None, :]
    col = lambda g: g.astype(F32)[:, None]
    per_layer = []
    for l in range(depth):
        per_layer.append(dict(
            g_mix=row(g_mix[l]), gq_a=col(gq_a[l]), gk_a=head_tile(gk_a[l]), gq_n=col(gq_n[l]),
            gk_n=head_tile(gk_n[l]),
            lam_q1=row(lam_q1[l]), lam_k1=row(lam_k1[l]), lam_q2=row(lam_q2[l]), lam_k2=row(lam_k2[l]),
            g_sub=col(g_sub[l]), shift=_score_bound(gq_a[l], gk_a[l]), g_ffn=row(g_ffn[l])))
    return stacked, per_layer


def kernel(x_prompt, x_sample, g_mix, w_in, gq_a, gk_a, lam_q1, lam_k1, lam_q2, lam_k2, g_sub, gq_n, gk_n,
           na_bias, w_pa, w_pb, w_o, g_ffn, w_up, conv_w, conv_b, w_down):
    weights = _layer_weights(g_mix, w_in, gq_a, gk_a, lam_q1, lam_k1, lam_q2, lam_k2, g_sub, gq_n, gk_n,
                             w_pa, w_pb, w_o, g_ffn, w_up, conv_w, conv_b, w_down)
    na_shift = _na_score_bound(gq_n, gk_n, na_bias)
    return tuple(_trunk(x, weights, na_bias, na_shift) for x in (x_prompt, x_sample))
```

```python
import functools
import math

import numpy as np
import jax
import jax.numpy as jnp
from jax import lax
from jax.experimental import pallas as pl
from jax.experimental.pallas import tpu as pltpu

D_MODEL = 1024
HEAD_DIM = 64
N_HEADS_DIFF = 4
N_HEADS_NA = 8
W_DIFF = N_HEADS_DIFF * 2 * HEAD_DIM
W_NA = N_HEADS_NA * HEAD_DIM
ATT_COLS = 3 * W_DIFF + 3 * W_NA
GRID_W = 64
WIN_H = 8
WIN_W = 16
D_FF = 2816
ROPE_THETA = 10000.0
EPS = 1e-6
NEG = -1e30

NA_QROWS = 4
NA_KROWS = 12
FFN_CHUNKS = (768, 768, 768, 512)
HALO = 8

VMEM_LIMIT = 56 * 1024 * 1024

F32 = jnp.float32
BF16 = jnp.bfloat16


def _const_spec(shape, index=None):
    index = (0,) * len(shape) if index is None else tuple(index)
    return pl.BlockSpec(shape, lambda *_: index, pipeline_mode=pl.Buffered(1))


def _params(sem):
    return pltpu.CompilerParams(dimension_semantics=sem, vmem_limit_bytes=VMEM_LIMIT)


def _rms(x, g):
    return x * lax.rsqrt(jnp.mean(x * x, axis=-1, keepdims=True) + EPS) * g


def _proj_kernel(x_ref, gmix_ref, w_ref, gqa_ref, gka_ref, gqn_ref, gkn_ref, cos_ref, sin_ref, cost_ref, sint_ref,
                 gsum_ref, qt_ref, k_ref, vt_ref, qnt_ref, kn_ref, vnt_ref):
    x = x_ref[0]
    h = _rms(x, gmix_ref[...]).astype(BF16)
    gsum = gsum_ref[...]
    half = HEAD_DIM // 2

    def proj(c0):
        return jnp.dot(h, w_ref[:, c0:c0 + W_DIFF], preferred_element_type=F32)

    def head_norm_t(zt, gcol):
        heads = []
        for r in range(0, W_DIFF, HEAD_DIM):
            blk = zt[r:r + HEAD_DIM]
            heads.append(blk * lax.rsqrt(jnp.mean(blk * blk, axis=0, keepdims=True) + EPS) * gcol)
        return heads

    def rope_t(blk):
        x1, x2 = blk[:half], blk[half:]
        return jnp.concatenate([x1 * cost_ref[...] - x2 * sint_ref[...], x1 * sint_ref[...] + x2 * cost_ref[...]],
                               axis=0)

    def head_norm(z, g):
        z2 = z * z
        hi = z2.astype(BF16)
        lo = (z2 - hi.astype(F32)).astype(BF16)
        parts = []
        for c in range(0, W_DIFF, 256):
            parts.append(jnp.dot(hi[:, c:c + 256], gsum, preferred_element_type=F32)
                         + jnp.dot(lo[:, c:c + 256], gsum, preferred_element_type=F32))
        ss = jnp.concatenate(parts, axis=-1)
        return z * lax.rsqrt(ss * (1.0 / HEAD_DIM) + EPS) * g

    cos = jnp.concatenate([cos_ref[...]] * 4, axis=-1)
    sin = jnp.concatenate([sin_ref[...]] * 4, axis=-1)
    lane = lax.broadcasted_iota(jnp.int32, (1, W_DIFF), 1)
    first_half = (lane % HEAD_DIM) < (HEAD_DIM // 2)

    def rope(y):
        partner = jnp.where(first_half, pltpu.roll(y, W_DIFF - HEAD_DIM // 2, 1), pltpu.roll(y, HEAD_DIM // 2, 1))
        return y * cos + partner * sin

    scale = HEAD_DIM ** -0.5 * LOG2E

    def emit_qa(z):
        qa = [rope_t(blk) for blk in head_norm_t(z.T, gqa_ref[...])]
        qt_ref[0] = (jnp.concatenate(qa, axis=0) * scale).astype(BF16)

    def emit_ka(z):
        k_ref[0] = rope(head_norm(z, gka_ref[...])).astype(BF16)

    def emit_va(z):
        vt_ref[0] = z.T.astype(BF16)

    def emit_qn(z):
        qnt_ref[0] = (jnp.concatenate(head_norm_t(z.T, gqn_ref[...]), axis=0) * scale).astype(BF16)

    def emit_kn(z):
        kn_ref[0] = head_norm(z, gkn_ref[...]).astype(BF16)

    def emit_vn(z):
        vnt_ref[0] = z.T.astype(BF16)

    emitters = (emit_qa, emit_ka, emit_va, emit_qn, emit_kn, emit_vn)
    z_next = proj(0)
    for i, emit in enumerate(emitters):
        z = z_next
        if i + 1 < len(emitters):
            z_next = proj((i + 1) * W_DIFF)
        emit(z)


def _proj(x, layer, gmix, w_in, gqa, gka, gqn, gkn, cos, sin, cost, sint, gsum, tm):
    b, s, _ = x.shape
    tok = lambda i, j: (i, j, 0)
    tr = lambda i, j: (i, 0, j)
    row_major = jax.ShapeDtypeStruct((b, s, W_DIFF), BF16)
    col_major = jax.ShapeDtypeStruct((b, W_DIFF, s), BF16)
    half = HEAD_DIM // 2
    return pl.pallas_call(
        _proj_kernel,
        grid=(b, s // tm),
        in_specs=[
            pl.BlockSpec((1, tm, D_MODEL), tok),
            _const_spec((1, D_MODEL)),
            _const_spec((None, D_MODEL, ATT_COLS), (layer, 0, 0)),
            _const_spec((HEAD_DIM, 1)), _const_spec((1, W_DIFF)), _const_spec((HEAD_DIM, 1)), _const_spec((1, W_NA)),
            pl.BlockSpec((tm, 128), lambda i, j: (j, 0)),
            pl.BlockSpec((tm, 128), lambda i, j: (j, 0)),
            pl.BlockSpec((half, tm), lambda i, j: (0, j)),
            pl.BlockSpec((half, tm), lambda i, j: (0, j)),
            _const_spec((256, 256)),
        ],
        out_specs=[
            pl.BlockSpec((1, W_DIFF, tm), tr),
            pl.BlockSpec((1, tm, W_DIFF), tok),
            pl.BlockSpec((1, W_DIFF, tm), tr),
            pl.BlockSpec((1, W_NA, tm), tr),
            pl.BlockSpec((1, tm, W_NA), tok),
            pl.BlockSpec((1, W_NA, tm), tr),
        ],
        out_shape=[col_major, row_major, col_major, col_major, row_major, col_major],
        compiler_params=_params(("parallel", "parallel")),
        name="proj",
    )(x, gmix, w_in, gqa, gka, gqn, gkn, cos, sin, cost, sint, gsum)


N_MAPS = 2 * N_HEADS_DIFF
LOG2E = math.log2(math.e)
MAX_STATIC_SHIFT = 60.0


def _masked_q_group(qt_ref, j, row):
    grp = qt_ref[0, 128 * (j // 2):128 * (j // 2) + 128, :]
    keep = (row >= HEAD_DIM) if (j % 2) else (row < HEAD_DIM)
    return jnp.where(keep, grp, jnp.zeros_like(grp))


def _diff_finish(acc_sc, l_of, lq1_ref, lk1_ref, lq2_ref, lk2_ref, gsub_ref, o_ref, lambda_init):
    lam = (jnp.exp(jnp.sum(lq1_ref[...] * lk1_ref[...])) - jnp.exp(jnp.sum(lq2_ref[...] * lk2_ref[...]))
           + lambda_init)
    for h in range(N_HEADS_DIFF):
        o0 = acc_sc[h] / l_of(h)
        o1 = acc_sc[N_HEADS_DIFF + h] / l_of(N_HEADS_DIFF + h)
        o = o0 - lam * o1
        o = o * lax.rsqrt(jnp.mean(o * o, axis=0, keepdims=True) + EPS) * gsub_ref[...] * (1.0 - lambda_init)
        o_ref[0, :, 128 * h:128 * h + 128] = o.T.astype(o_ref.dtype)


def _diff_online_kernel(shift_ref, qt_ref, k_ref, vt_ref, lq1_ref, lk1_ref, lq2_ref, lk2_ref, gsub_ref, o_ref,
                        qz_sc, m_sc, l_sc, acc_sc, *, lambda_init):
    del shift_ref
    kv = pl.program_id(2)
    tq = qt_ref.shape[2]

    @pl.when(kv == 0)
    def _init():
        m_sc[...] = jnp.full(m_sc.shape, NEG, F32)
        l_sc[...] = jnp.zeros(l_sc.shape, F32)
        acc_sc[...] = jnp.zeros(acc_sc.shape, F32)
        row = lax.broadcasted_iota(jnp.int32, (128, tq), 0)
        for j in range(N_MAPS):
            qz_sc[j] = _masked_q_group(qt_ref, j, row)

    for j in range(N_MAPS):
        h = j % N_HEADS_DIFF
        kp = k_ref[0, :, 128 * (j // 2):128 * (j // 2) + 128]
        s = jnp.dot(kp, qz_sc[j], preferred_element_type=F32)
        m_prev = m_sc[j]
        m_cur = jnp.maximum(m_prev, jnp.max(s, axis=0, keepdims=True))
        alpha = jnp.exp2(m_prev - m_cur)
        p = jnp.exp2(s - m_cur)
        l_sc[j] = alpha * l_sc[j] + jnp.sum(p, axis=0, keepdims=True)
        vt = vt_ref[0, 128 * h:128 * h + 128, :]
        acc_sc[j] = alpha * acc_sc[j] + jnp.dot(vt, p.astype(BF16), preferred_element_type=F32)
        m_sc[j] = m_cur

    @pl.when(kv == pl.num_programs(2) - 1)
    def _finish():
        _diff_finish(acc_sc, lambda j: l_sc[j], lq1_ref, lk1_ref, lq2_ref, lk2_ref, gsub_ref, o_ref, lambda_init)


def _diff_static_kernel(shift_ref, qt_ref, k_ref, vt_ref, lq1_ref, lk1_ref, lq2_ref, lk2_ref, gsub_ref, o_ref,
                        qz_sc, l_sc, acc_sc, *, lambda_init):
    kv = pl.program_id(2)
    tq = qt_ref.shape[2]
    tk = k_ref.shape[1]

    @pl.when(kv == 0)
    def _init():
        l_sc[...] = jnp.zeros(l_sc.shape, F32)
        acc_sc[...] = jnp.zeros(acc_sc.shape, F32)
        row = lax.broadcasted_iota(jnp.int32, (128, tq), 0)
        shift_rows = jnp.where(row == 0, -shift_ref[0], 0.0).astype(BF16)
        for j in range(N_MAPS):
            qz_sc[j, :128, :] = _masked_q_group(qt_ref, j, row)
            qz_sc[j, 128:, :] = shift_rows

    ones = jnp.ones((tk, 128), BF16)

    def scores(j):
        kaug = jnp.concatenate([k_ref[0, :, 128 * (j // 2):128 * (j // 2) + 128], ones], axis=1)
        return jnp.dot(kaug, qz_sc[j], preferred_element_type=F32)

    s_next = scores(0)
    for j in range(N_MAPS):
        h = j % N_HEADS_DIFF
        s = s_next
        if j + 1 < N_MAPS:
            s_next = scores(j + 1)
        p = jnp.exp2(s)
        l_sc[j] += jnp.sum(p.reshape(tk // 8, 8, tq), axis=0)
        vt = vt_ref[0, 128 * h:128 * h + 128, :]
        acc_sc[j] += jnp.dot(vt, p.astype(BF16), preferred_element_type=F32)

    @pl.when(kv == pl.num_programs(2) - 1)
    def _finish():
        _diff_finish(acc_sc, lambda j: jnp.sum(l_sc[j], axis=0, keepdims=True),
                     lq1_ref, lk1_ref, lq2_ref, lk2_ref, gsub_ref, o_ref, lambda_init)


def _diff_call(body, scratch, operands, lambda_init, tq, tk):
    qt = operands[1]
    b, _, s = qt.shape
    return pl.pallas_call(
        functools.partial(body, lambda_init=lambda_init),
        grid=(b, s // tq, s // tk),
        in_specs=[
            pl.BlockSpec(memory_space=pltpu.SMEM),
            pl.BlockSpec((1, W_DIFF, tq), lambda i, j, t: (i, 0, j)),
            pl.BlockSpec((1, tk, W_DIFF), lambda i, j, t: (i, t, 0)),
            pl.BlockSpec((1, W_DIFF, tk), lambda i, j, t: (i, 0, t)),
            _const_spec((1, HEAD_DIM)), _const_spec((1, HEAD_DIM)), _const_spec((1, HEAD_DIM)),
            _const_spec((1, HEAD_DIM)),
            _const_spec((2 * HEAD_DIM, 1)),
        ],
        out_specs=pl.BlockSpec((1, tq, W_DIFF), lambda i, j, t: (i, j, 0)),
        out_shape=jax.ShapeDtypeStruct((b, s, W_DIFF), BF16),
        scratch_shapes=scratch,
        compiler_params=_params(("parallel", "parallel", "arbitrary")),
        name=body.__name__.strip("_"),
    )(*operands)


def _diff_attention(shift, qt, k, vt, lq1, lk1, lq2, lk2, gsub, lambda_init, tq, tk):
    operands = (shift, qt, k, vt, lq1, lk1, lq2, lk2, gsub)
    static = functools.partial(
        _diff_call, _diff_static_kernel,
        [pltpu.VMEM((N_MAPS, 256, tq), BF16), pltpu.VMEM((N_MAPS, 8, tq), F32), pltpu.VMEM((N_MAPS, 128, tq), F32)],
        lambda_init=lambda_init, tq=tq, tk=tk)
    online = functools.partial(
        _diff_call, _diff_online_kernel,
        [pltpu.VMEM((N_MAPS, 128, tq), BF16), pltpu.VMEM((N_MAPS, 1, tq), F32), pltpu.VMEM((N_MAPS, 1, tq), F32),
         pltpu.VMEM((N_MAPS, 128, tq), F32)],
        lambda_init=lambda_init, tq=tq, tk=tk)
    return lax.cond(shift[0] <= MAX_STATIC_SHIFT, static, online, operands)


def _na_selectors(rows):
    kh = min(WIN_H, rows)
    assert kh == WIN_H and rows >= NA_KROWS + NA_QROWS
    rsel = np.zeros((3, NA_QROWS, NA_KROWS, 2 * WIN_H - 1), np.float32)
    for v in range(3):
        r0 = (0, NA_QROWS, rows - NA_QROWS)[v]
        ws = min(max(r0 - 4, 0), rows - NA_KROWS)
        for t in range(NA_QROWS):
            r = r0 + t
            rs = min(max(r - kh // 2, 0), rows - kh)
            for i in range(NA_KROWS):
                kr = ws + i
                if rs <= kr < rs + kh:
                    rsel[v, t, i, kr - r + WIN_H - 1] = 1.0
    csel = np.zeros((GRID_W, GRID_W, 2 * WIN_W - 1), np.float32)
    for c in range(GRID_W):
        cs = min(max(c - WIN_W // 2, 0), GRID_W - WIN_W)
        for j in range(cs, cs + WIN_W):
            csel[c, j, j - c + WIN_W - 1] = 1.0
    return rsel, csel


def _na_bias_blocks(na_bias, rows, shifts):
    rsel, csel = _na_selectors(rows)
    slab = jnp.einsum("lhde,cje->lhdjc", na_bias.astype(F32), csel, precision=lax.Precision.HIGHEST)
    slab = slab * LOG2E - shifts.reshape(-1, 1, 1, 1, 1)
    slab = jnp.where((csel.sum(-1) > 0.5).T, slab, NEG)
    slab = jnp.concatenate([slab, jnp.full_like(slab[:, :, :1], NEG)], axis=2)
    rel_row = np.where(rsel.sum(-1) > 0.5, rsel.argmax(-1), 2 * WIN_H - 1)
    variants = []
    for v in range(3):
        key_rows = [jnp.concatenate([slab[:, :, rel_row[v, t, i]] for t in range(NA_QROWS)], axis=-1)
                    for i in range(NA_KROWS)]
        variants.append(jnp.concatenate(key_rows, axis=2))
    return jnp.stack(variants, axis=1)


def _na_kernel(qt_ref, k0_ref, k1_ref, k2_ref, v0_ref, v1_ref, v2_ref, bias_ref, o_ref, *, subtract_max):
    tq = qt_ref.shape[2]
    kwin = jnp.concatenate([k0_ref[0], k1_ref[0], k2_ref[0]], axis=0)
    vtwin = jnp.concatenate([v0_ref[0], v1_ref[0], v2_ref[0]], axis=1)
    row = lax.broadcasted_iota(jnp.int32, (128, tq), 0)

    def scores(h):
        return jnp.dot(kwin[:, 128 * (h // 2):128 * (h // 2) + 128], _masked_q_group(qt_ref, h, row),
                       preferred_element_type=F32) + bias_ref[0, h]

    outs = []
    s_next = scores(0)
    for h in range(N_HEADS_NA):
        s = s_next
        if h + 1 < N_HEADS_NA:
            s_next = scores(h + 1)
        if subtract_max:
            s = s - jnp.max(s, axis=0, keepdims=True)
        p = jnp.exp2(s)
        l = jnp.sum(p, axis=0, keepdims=True)
        vt = vtwin[HEAD_DIM * h:HEAD_DIM * (h + 1), :]
        outs.append(jnp.dot(vt, p.astype(BF16), preferred_element_type=F32) / l)
    o_ref[0] = jnp.concatenate(outs, axis=0).T.astype(o_ref.dtype)


def _na_call(subtract_max, operands):
    qnt, kn, vnt, bias_blocks = operands
    b, s, _ = kn.shape
    rows = s // GRID_W
    nblk = rows // NA_QROWS
    tq = NA_QROWS * GRID_W
    first = lambda j: jnp.clip(j - 1, 0, nblk - 3)
    variant = lambda i, j: (jnp.where(j == 0, 0, jnp.where(j == nblk - 1, 2, 1)), 0, 0, 0)
    kblk = lambda d: pl.BlockSpec((1, tq, W_NA), lambda i, j: (i, first(j) + d, 0))
    vblk = lambda d: pl.BlockSpec((1, W_NA, tq), lambda i, j: (i, 0, first(j) + d))
    return pl.pallas_call(
        functools.partial(_na_kernel, subtract_max=subtract_max),
        grid=(b, nblk),
        in_specs=[pl.BlockSpec((1, W_NA, tq), lambda i, j: (i, 0, j)),
                  kblk(0), kblk(1), kblk(2), vblk(0), vblk(1), vblk(2),
                  pl.BlockSpec((1, N_HEADS_NA, NA_KROWS * GRID_W, tq), variant)],
        out_specs=pl.BlockSpec((1, tq, W_NA), lambda i, j: (i, j, 0)),
        out_shape=jax.ShapeDtypeStruct((b, s, W_NA), BF16),
        compiler_params=_params(("parallel", "arbitrary")),
        name="natt_online" if subtract_max else "natt_static",
    )(qnt, kn, kn, kn, vnt, vnt, vnt, bias_blocks)


def _na_attention(shift_is_static, qnt, kn, vnt, bias_blocks):
    return lax.cond(shift_is_static, functools.partial(_na_call, False), functools.partial(_na_call, True),
                    (qnt, kn, vnt, bias_blocks))


def _merge_kernel(x_ref, oa_ref, on_ref, gmix_ref, wga_ref, wgb_ref, wpa_ref, wpb_ref, wo_ref, y_ref):
    x = x_ref[0]
    h = _rms(x, gmix_ref[...]).astype(BF16)
    ga = jnp.dot(h, wga_ref[...], preferred_element_type=F32)
    gb = jnp.dot(h, wgb_ref[...], preferred_element_type=F32)
    pa = jnp.dot(oa_ref[0], wpa_ref[...], preferred_element_type=F32)
    pb = jnp.dot(on_ref[0], wpb_ref[...], preferred_element_type=F32)
    mixed = jax.nn.sigmoid(ga) * pa + jax.nn.sigmoid(gb) * pb
    y_ref[0] = x + jnp.dot(mixed.astype(BF16), wo_ref[...], preferred_element_type=F32)


def _merge(x, oa, on, layer, gmix, w_in, w_pa, w_pb, w_o, tm):
    b, s, _ = x.shape
    tok = lambda i, j: (i, j, 0)
    gate_block = ATT_COLS // D_MODEL
    return pl.pallas_call(
        _merge_kernel,
        grid=(b, s // tm),
        in_specs=[
            pl.BlockSpec((1, tm, D_MODEL), tok),
            pl.BlockSpec((1, tm, W_DIFF), tok),
            pl.BlockSpec((1, tm, W_NA), tok),
            _const_spec((1, D_MODEL)),
            _const_spec((None, D_MODEL, D_MODEL), (layer, 0, gate_block)),
            _const_spec((None, D_MODEL, D_MODEL), (layer, 0, gate_block + 1)),
            _const_spec((None, W_DIFF, D_MODEL), (layer, 0, 0)),
            _const_spec((None, W_NA, D_MODEL), (layer, 0, 0)),
            _const_spec((None, D_MODEL, D_MODEL), (layer, 0, 0)),
        ],
        out_specs=pl.BlockSpec((1, tm, D_MODEL), tok),
        out_shape=jax.ShapeDtypeStruct(x.shape, F32),
        compiler_params=_params(("parallel", "parallel")),
        name="merge",
    )(x, oa, on, gmix, w_in, w_in, w_pa, w_pb, w_o)


def _ffn_kernel(x_ref, xp_ref, xn_ref, g_ref, wup_ref, cw_ref, cb_ref, wdn_ref, y_ref):
    j = pl.program_id(1)
    tm = x_ref.shape[1]
    x = x_ref[0]
    xe = jnp.concatenate([xp_ref[0], x, xn_ref[0]], axis=0)
    row = lax.broadcasted_iota(jnp.int32, (tm + 2 * HALO, 1), 0)
    inside = jnp.logical_and(jnp.logical_or(row >= HALO, j > 0),
                             jnp.logical_or(row < tm + HALO, j < pl.num_programs(1) - 1))
    he = jnp.where(inside, _rms(xe, g_ref[...]), 0.0).astype(BF16)

    def up(c0, n):
        return tuple(jnp.dot(he, wup_ref[:, c:c + n], preferred_element_type=F32)
                     for c in (c0, D_FF + c0))

    def conv(u, c0, n):
        w = cw_ref[:, c0:c0 + n]
        prev = pltpu.roll(u, 1, 0)[HALO:HALO + tm]
        nxt = pltpu.roll(u, tm + 2 * HALO - 1, 0)[HALO:HALO + tm]
        return prev * w[0:1] + u[HALO:HALO + tm] * w[1:2] + nxt * w[2:3] + cb_ref[:, c0:c0 + n]

    def gelu(v):
        c = math.sqrt(2.0 / math.pi)
        half_v = 0.5 * v
        return half_v + half_v * jnp.tanh(v * (c + (c * 0.044715) * (v * v)))

    starts = [sum(FFN_CHUNKS[:i]) for i in range(len(FFN_CHUNKS))]
    acc = x
    u_next = up(starts[0], FFN_CHUNKS[0])
    for i, (c0, n) in enumerate(zip(starts, FFN_CHUNKS)):
        ug, uv = u_next
        if i + 1 < len(FFN_CHUNKS):
            u_next = up(starts[i + 1], FFN_CHUNKS[i + 1])
        act = (gelu(conv(ug, c0, n)) * conv(uv, D_FF + c0, n)).astype(BF16)
        acc = acc + jnp.dot(act, wdn_ref[c0:c0 + n, :], preferred_element_type=F32)
    y_ref[0] = acc


def _ffn(x, layer, g, w_up, conv_w, conv_b, w_down, tm):
    b, s, _ = x.shape
    nb = tm // HALO
    last = s // HALO - 1
    return pl.pallas_call(
        _ffn_kernel,
        grid=(b, s // tm),
        in_specs=[
            pl.BlockSpec((1, tm, D_MODEL), lambda i, j: (i, j, 0)),
            pl.BlockSpec((1, HALO, D_MODEL), lambda i, j: (i, jnp.maximum(j * nb - 1, 0), 0)),
            pl.BlockSpec((1, HALO, D_MODEL), lambda i, j: (i, jnp.minimum((j + 1) * nb, last), 0)),
            _const_spec((1, D_MODEL)),
            _const_spec((None, D_MODEL, 2 * D_FF), (layer, 0, 0)),
            _const_spec((None, 3, 2 * D_FF), (layer, 0, 0)),
            _const_spec((None, 1, 2 * D_FF), (layer, 0, 0)),
            _const_spec((None, D_FF, D_MODEL), (layer, 0, 0)),
        ],
        out_specs=pl.BlockSpec((1, tm, D_MODEL), lambda i, j: (i, j, 0)),
        out_shape=jax.ShapeDtypeStruct(x.shape, F32),
        compiler_params=_params(("parallel", "parallel")),
        name="ffn",
    )(x, x, x, g, w_up, conv_w, conv_b, w_down)


def _rope_tables(seq):
    half = HEAD_DIM // 2
    inv = jnp.power(ROPE_THETA, -jnp.arange(half, dtype=F32) * 2.0 / HEAD_DIM)
    ang = jnp.arange(seq, dtype=F32)[:, None] * inv[None, :]
    cos, sin = jnp.cos(ang), jnp.sin(ang)
    return jnp.tile(cos, (1, 4)), jnp.tile(jnp.concatenate([-sin, sin], axis=-1), (1, 2)), cos.T, sin.T


def _group_sum_matrix():
    idx = np.arange(256) // HEAD_DIM
    return jnp.asarray(idx[:, None] == idx[None, :], dtype=BF16)


def _tiles(seq):
    tm = min(1024, seq)
    tm_ffn = min(512, seq)
    tq = min(1024, seq)
    tk = min(2048, seq)
    return tm, tm_ffn, tq, tk


def _trunk(x, weights, na_bias, na_shift):
    stacked, per_layer = weights
    seq = x.shape[1]
    tm, tm_ffn, tq, tk = _tiles(seq)
    cos, sin, cost, sint = _rope_tables(seq)
    gsum = _group_sum_matrix()
    na_static = na_shift <= MAX_STATIC_SHIFT
    bias_blocks = _na_bias_blocks(na_bias, seq // GRID_W, jnp.where(na_static, na_shift, 0.0))
    for l, w in enumerate(per_layer):
        lambda_init = 0.8 - 0.6 * math.exp(-0.3 * l)
        qt, k, vt, qnt, kn, vnt = _proj(x, l, w["g_mix"], stacked["w_in"], w["gq_a"], w["gk_a"], w["gq_n"],
                                        w["gk_n"], cos, sin, cost, sint, gsum, tm)
        oa = _diff_attention(w["shift"], qt, k, vt, w["lam_q1"], w["lam_k1"], w["lam_q2"], w["lam_k2"], w["g_sub"],
                             lambda_init, tq, tk)
        on = _na_attention(na_static[l], qnt, kn, vnt, bias_blocks[l])
        x = _merge(x, oa, on, l, w["g_mix"], stacked["w_in"], stacked["w_pa"], stacked["w_pb"], stacked["w_o"], tm)
        x = _ffn(x, l, w["g_ffn"], stacked["w_up"], stacked["conv_w"], stacked["conv_b"], stacked["w_down"],
                 tm_ffn)
    return x


def _score_bound(gq, gk):
    bound = (HEAD_DIM ** 0.5) * LOG2E * jnp.max(jnp.abs(gq)) * jnp.max(jnp.abs(gk))
    return jnp.ceil(1.02 * bound.astype(F32)).reshape(1)


def _na_score_bound(gq, gk, na_bias):
    amax = lambda a: jnp.max(jnp.abs(a.astype(F32)).reshape(a.shape[0], -1), axis=1)
    bound = LOG2E * ((HEAD_DIM ** 0.5) * amax(gq) * amax(gk) + amax(na_bias))
    return jnp.ceil(1.02 * bound)


def _layer_weights(g_mix, w_in, gq_a, gk_a, lam_q1, lam_k1, lam_q2, lam_k2, g_sub, gq_n, gk_n,
                   w_pa, w_pb, w_o, g_ffn, w_up, conv_w, conv_b, w_down):
    depth = w_in.shape[0]
    stacked = dict(w_in=w_in.astype(BF16), w_pa=w_pa.astype(BF16), w_pb=w_pb.astype(BF16), w_o=w_o.astype(BF16),
                   w_up=w_up.astype(BF16), w_down=w_down.astype(BF16), conv_w=conv_w.astype(F32),
                   conv_b=conv_b.astype(F32)[:, None, :])
    head_tile = lambda g: jnp.tile(g.astype(F32), W_DIFF // HEAD_DIM)[None, :]
    row = lambda g: g.astype(F32)[None, :]
    col = lambda g: g.astype(F32)[:, None]
    per_layer = []
    for l in range(depth):
        per_layer.append(dict(
            g_mix=row(g_mix[l]), gq_a=col(gq_a[l]), gk_a=head_tile(gk_a[l]), gq_n=col(gq_n[l]),
            gk_n=head_tile(gk_n[l]),
            lam_q1=row(lam_q1[l]), lam_k1=row(lam_k1[l]), lam_q2=row(lam_q2[l]), lam_k2=row(lam_k2[l]),
            g_sub=col(g_sub[l]), shift=_score_bound(gq_a[l], gk_a[l]), g_ffn=row(g_ffn[l])))
    return stacked, per_layer


def kernel(x_prompt, x_sample, g_mix, w_in, gq_a, gk_a, lam_q1, lam_k1, lam_q2, lam_k2, g_sub, gq_n, gk_n,
           na_bias, w_pa, w_pb, w_o, g_ffn, w_up, conv_w, conv_b, w_down):
    weights = _layer_weights(g_mix, w_in, gq_a, gk_a, lam_q1, lam_k1, lam_q2, lam_k2, g_sub, gq_n, gk_n,
                             w_pa, w_pb, w_o, g_ffn, w_up, conv_w, conv_b, w_down)
    na_shift = _na_score_bound(gq_n, gk_n, na_bias)
    return tuple(_trunk(x, weights, na_bias, na_shift) for x in (x_prompt, x_sample))
```

```python
import functools
import math

import numpy as np
import jax
import jax.numpy as jnp
from jax import lax
from jax.experimental import pallas as pl
from jax.experimental.pallas import tpu as pltpu

D_MODEL = 1024
HEAD_DIM = 64
N_HEADS_DIFF = 4
N_HEADS_NA = 8
W_DIFF = N_HEADS_DIFF * 2 * HEAD_DIM
W_NA = N_HEADS_NA * HEAD_DIM
ATT_COLS = 3 * W_DIFF + 3 * W_NA
GRID_W = 64
WIN_H = 8
WIN_W = 16
D_FF = 2816
ROPE_THETA = 10000.0
EPS = 1e-6
NEG = -1e30

NA_QROWS = 4
NA_KROWS = 12
FFN_CHUNKS = (768, 768, 768, 512)
HALO = 8

VMEM_LIMIT = 56 * 1024 * 1024

F32 = jnp.float32
BF16 = jnp.bfloat16


def _const_spec(shape, index=None):
    index = (0,) * len(shape) if index is None else tuple(index)
    return pl.BlockSpec(shape, lambda *_: index, pipeline_mode=pl.Buffered(1))


def _params(sem):
    return pltpu.CompilerParams(dimension_semantics=sem, vmem_limit_bytes=VMEM_LIMIT)


def _rms(x, g):
    return x * lax.rsqrt(jnp.mean(x * x, axis=-1, keepdims=True) + EPS) * g


def _proj_kernel(x_ref, gmix_ref, w_ref, gqa_ref, gka_ref, gqn_ref, gkn_ref, cos_ref, sin_ref, cost_ref, sint_ref,
                 gsum_ref, qt_ref, k_ref, vt_ref, qnt_ref, kn_ref, vnt_ref):
    x = x_ref[0]
    h = _rms(x, gmix_ref[...]).astype(BF16)
    gsum = gsum_ref[...]
    half = HEAD_DIM // 2

    def proj(c0):
        return jnp.dot(h, w_ref[:, c0:c0 + W_DIFF], preferred_element_type=F32)

    def head_norm_t(zt, gcol):
        heads = []
        for r in range(0, W_DIFF, HEAD_DIM):
            blk = zt[r:r + HEAD_DIM]
            heads.append(blk * lax.rsqrt(jnp.mean(blk * blk, axis=0, keepdims=True) + EPS) * gcol)
        return heads

    def rope_t(blk):
        x1, x2 = blk[:half], blk[half:]
        return jnp.concatenate([x1 * cost_ref[...] - x2 * sint_ref[...], x1 * sint_ref[...] + x2 * cost_ref[...]],
                               axis=0)

    def head_norm(z, g):
        z2 = z * z
        hi = z2.astype(BF16)
        lo = (z2 - hi.astype(F32)).astype(BF16)
        parts = []
        for c in range(0, W_DIFF, 256):
            parts.append(jnp.dot(hi[:, c:c + 256], gsum, preferred_element_type=F32)
                         + jnp.dot(lo[:, c:c + 256], gsum, preferred_element_type=F32))
        ss = jnp.concatenate(parts, axis=-1)
        return z * lax.rsqrt(ss * (1.0 / HEAD_DIM) + EPS) * g

    cos = jnp.concatenate([cos_ref[...]] * 4, axis=-1)
    sin = jnp.concatenate([sin_ref[...]] * 4, axis=-1)
    lane = lax.broadcasted_iota(jnp.int32, (1, W_DIFF), 1)
    first_half = (lane % HEAD_DIM) < (HEAD_DIM // 2)

    def rope(y):
        partner = jnp.where(first_half, pltpu.roll(y, W_DIFF - HEAD_DIM // 2, 1), pltpu.roll(y, HEAD_DIM // 2, 1))
        return y * cos + partner * sin

    scale = HEAD_DIM ** -0.5 * LOG2E

    def emit_qa(z):
        qa = [rope_t(blk) for blk in head_norm_t(z.T, gqa_ref[...])]
        qt_ref[0] = (jnp.concatenate(qa, axis=0) * scale).astype(BF16)

    def emit_ka(z):
        k_ref[0] = rope(head_norm(z, gka_ref[...])).astype(BF16)

    def emit_va(z):
        vt_ref[0] = z.T.astype(BF16)

    def emit_qn(z):
        qnt_ref[0] = (jnp.concatenate(head_norm_t(z.T, gqn_ref[...]), axis=0) * scale).astype(BF16)

    def emit_kn(z):
        kn_ref[0] = head_norm(z, gkn_ref[...]).astype(BF16)

    def emit_vn(z):
        vnt_ref[0] = z.T.astype(BF16)

    emitters = (emit_qa, emit_ka, emit_va, emit_qn, emit_kn, emit_vn)
    z_next = proj(0)
    for i, emit in enumerate(emitters):
        z = z_next
        if i + 1 < len(emitters):
            z_next = proj((i + 1) * W_DIFF)
        emit(z)


def _proj(x, layer, gmix, w_in, gqa, gka, gqn, gkn, cos, sin, cost, sint, gsum, tm):
    b, s, _ = x.shape
    tok = lambda i, j: (i, j, 0)
    tr = lambda i, j: (i, 0, j)
    row_major = jax.ShapeDtypeStruct((b, s, W_DIFF), BF16)
    col_major = jax.ShapeDtypeStruct((b, W_DIFF, s), BF16)
    half = HEAD_DIM // 2
    return pl.pallas_call(
        _proj_kernel,
        grid=(b, s // tm),
        in_specs=[
            pl.BlockSpec((1, tm, D_MODEL), tok),
            _const_spec((1, D_MODEL)),
            _const_spec((None, D_MODEL, ATT_COLS), (layer, 0, 0)),
            _const_spec((HEAD_DIM, 1)), _const_spec((1, W_DIFF)), _const_spec((HEAD_DIM, 1)), _const_spec((1, W_NA)),
            pl.BlockSpec((tm, 128), lambda i, j: (j, 0)),
            pl.BlockSpec((tm, 128), lambda i, j: (j, 0)),
            pl.BlockSpec((half, tm), lambda i, j: (0, j)),
            pl.BlockSpec((half, tm), lambda i, j: (0, j)),
            _const_spec((256, 256)),
        ],
        out_specs=[
            pl.BlockSpec((1, W_DIFF, tm), tr),
            pl.BlockSpec((1, tm, W_DIFF), tok),
            pl.BlockSpec((1, W_DIFF, tm), tr),
            pl.BlockSpec((1, W_NA, tm), tr),
            pl.BlockSpec((1, tm, W_NA), tok),
            pl.BlockSpec((1, W_NA, tm), tr),
        ],
        out_shape=[col_major, row_major, col_major, col_major, row_major, col_major],
        compiler_params=_params(("parallel", "parallel")),
        name="proj",
    )(x, gmix, w_in, gqa, gka, gqn, gkn, cos, sin, cost, sint, gsum)


N_MAPS = 2 * N_HEADS_DIFF
LOG2E = math.log2(math.e)
MAX_STATIC_SHIFT = 60.0


def _masked_q_group(qt_ref, j, row):
    grp = qt_ref[0, 128 * (j // 2):128 * (j // 2) + 128, :]
    keep = (row >= HEAD_DIM) if (j % 2) else (row < HEAD_DIM)
    return jnp.where(keep, grp, jnp.zeros_like(grp))


def _diff_finish(acc_sc, l_of, lq1_ref, lk1_ref, lq2_ref, lk2_ref, gsub_ref, o_ref, lambda_init):
    lam = (jnp.exp(jnp.sum(lq1_ref[...] * lk1_ref[...])) - jnp.exp(jnp.sum(lq2_ref[...] * lk2_ref[...]))
           + lambda_init)
    for h in range(N_HEADS_DIFF):
        o0 = acc_sc[h] * (1.0 / l_of(h))
        o1 = acc_sc[N_HEADS_DIFF + h] * (lam / l_of(N_HEADS_DIFF + h))
        o = o0 - o1
        o = o * lax.rsqrt(jnp.mean(o * o, axis=0, keepdims=True) + EPS) * gsub_ref[...] * (1.0 - lambda_init)
        o_ref[0, :, 128 * h:128 * h + 128] = o.T.astype(o_ref.dtype)


def _diff_online_kernel(shift_ref, qt_ref, k_ref, vt_ref, lq1_ref, lk1_ref, lq2_ref, lk2_ref, gsub_ref, o_ref,
                        qz_sc, m_sc, l_sc, acc_sc, *, lambda_init):
    del shift_ref
    kv = pl.program_id(2)
    tq = qt_ref.shape[2]

    @pl.when(kv == 0)
    def _init():
        m_sc[...] = jnp.full(m_sc.shape, NEG, F32)
        l_sc[...] = jnp.zeros(l_sc.shape, F32)
        acc_sc[...] = jnp.zeros(acc_sc.shape, F32)
        row = lax.broadcasted_iota(jnp.int32, (128, tq), 0)
        for j in range(N_MAPS):
            qz_sc[j] = _masked_q_group(qt_ref, j, row)

    for j in range(N_MAPS):
        h = j % N_HEADS_DIFF
        kp = k_ref[0, :, 128 * (j // 2):128 * (j // 2) + 128]
        s = jnp.dot(kp, qz_sc[j], preferred_element_type=F32)
        m_prev = m_sc[j]
        m_cur = jnp.maximum(m_prev, jnp.max(s, axis=0, keepdims=True))
        alpha = jnp.exp2(m_prev - m_cur)
        p = jnp.exp2(s - m_cur)
        l_sc[j] = alpha * l_sc[j] + jnp.sum(p, axis=0, keepdims=True)
        vt = vt_ref[0, 128 * h:128 * h + 128, :]
        acc_sc[j] = alpha * acc_sc[j] + jnp.dot(vt, p.astype(BF16), preferred_element_type=F32)
        m_sc[j] = m_cur

    @pl.when(kv == pl.num_programs(2) - 1)
    def _finish():
        _diff_finish(acc_sc, lambda j: l_sc[j], lq1_ref, lk1_ref, lq2_ref, lk2_ref, gsub_ref, o_ref, lambda_init)


def _diff_static_kernel(shift_ref, qt_ref, k_ref, vt_ref, lq1_ref, lk1_ref, lq2_ref, lk2_ref, gsub_ref, o_ref,
                        qz_sc, l_sc, acc_sc, *, lambda_init):
    kv = pl.program_id(2)
    tq = qt_ref.shape[2]
    tk = k_ref.shape[1]

    @pl.when(kv == 0)
    def _init():
        l_sc[...] = jnp.zeros(l_sc.shape, F32)
        acc_sc[...] = jnp.zeros(acc_sc.shape, F32)
        row = lax.broadcasted_iota(jnp.int32, (128, tq), 0)
        shift_rows = jnp.where(row == 0, -shift_ref[0], 0.0).astype(BF16)
        for j in range(N_MAPS):
            qz_sc[j, :128, :] = _masked_q_group(qt_ref, j, row)
            qz_sc[j, 128:, :] = shift_rows

    ones = jnp.ones((tk, 128), BF16)

    def scores(j):
        kaug = jnp.concatenate([k_ref[0, :, 128 * (j // 2):128 * (j // 2) + 128], ones], axis=1)
        return jnp.dot(kaug, qz_sc[j], preferred_element_type=F32)

    s_next = scores(0)
    for j in range(N_MAPS):
        h = j % N_HEADS_DIFF
        s = s_next
        if j + 1 < N_MAPS:
            s_next = scores(j + 1)
        p = jnp.exp2(s)
        l_sc[j] += jnp.sum(p.reshape(tk // 8, 8, tq), axis=0)
        vt = vt_ref[0, 128 * h:128 * h + 128, :]
        acc_sc[j] += jnp.dot(vt, p.astype(BF16), preferred_element_type=F32)

    @pl.when(kv == pl.num_programs(2) - 1)
    def _finish():
        _diff_finish(acc_sc, lambda j: jnp.sum(l_sc[j], axis=0, keepdims=True),
                     lq1_ref, lk1_ref, lq2_ref, lk2_ref, gsub_ref, o_ref, lambda_init)


def _diff_call(body, scratch, operands, lambda_init, tq, tk):
    qt = operands[1]
    b, _, s = qt.shape
    return pl.pallas_call(
        functools.partial(body, lambda_init=lambda_init),
        grid=(b, s // tq, s // tk),
        in_specs=[
            pl.BlockSpec(memory_space=pltpu.SMEM),
            pl.BlockSpec((1, W_DIFF, tq), lambda i, j, t: (i, 0, j)),
            pl.BlockSpec((1, tk, W_DIFF), lambda i, j, t: (i, t, 0)),
            pl.BlockSpec((1, W_DIFF, tk), lambda i, j, t: (i, 0, t)),
            _const_spec((1, HEAD_DIM)), _const_spec((1, HEAD_DIM)), _const_spec((1, HEAD_DIM)),
            _const_spec((1, HEAD_DIM)),
            _const_spec((2 * HEAD_DIM, 1)),
        ],
        out_specs=pl.BlockSpec((1, tq, W_DIFF), lambda i, j, t: (i, j, 0)),
        out_shape=jax.ShapeDtypeStruct((b, s, W_DIFF), BF16),
        scratch_shapes=scratch,
        compiler_params=_params(("parallel", "parallel", "arbitrary")),
        name=body.__name__.strip("_"),
    )(*operands)


def _diff_attention(shift, qt, k, vt, lq1, lk1, lq2, lk2, gsub, lambda_init, tq, tk):
    operands = (shift, qt, k, vt, lq1, lk1, lq2, lk2, gsub)
    static = functools.partial(
        _diff_call, _diff_static_kernel,
        [pltpu.VMEM((N_MAPS, 256, tq), BF16), pltpu.VMEM((N_MAPS, 8, tq), F32), pltpu.VMEM((N_MAPS, 128, tq), F32)],
        lambda_init=lambda_init, tq=tq, tk=tk)
    online = functools.partial(
        _diff_call, _diff_online_kernel,
        [pltpu.VMEM((N_MAPS, 128, tq), BF16), pltpu.VMEM((N_MAPS, 1, tq), F32), pltpu.VMEM((N_MAPS, 1, tq), F32),
         pltpu.VMEM((N_MAPS, 128, tq), F32)],
        lambda_init=lambda_init, tq=tq, tk=tk)
    return lax.cond(shift[0] <= MAX_STATIC_SHIFT, static, online, operands)


def _na_selectors(rows):
    kh = min(WIN_H, rows)
    assert kh == WIN_H and rows >= NA_KROWS + NA_QROWS
    rsel = np.zeros((3, NA_QROWS, NA_KROWS, 2 * WIN_H - 1), np.float32)
    for v in range(3):
        r0 = (0, NA_QROWS, rows - NA_QROWS)[v]
        ws = min(max(r0 - 4, 0), rows - NA_KROWS)
        for t in range(NA_QROWS):
            r = r0 + t
            rs = min(max(r - kh // 2, 0), rows - kh)
            for i in range(NA_KROWS):
                kr = ws + i
                if rs <= kr < rs + kh:
                    rsel[v, t, i, kr - r + WIN_H - 1] = 1.0
    csel = np.zeros((GRID_W, GRID_W, 2 * WIN_W - 1), np.float32)
    for c in range(GRID_W):
        cs = min(max(c - WIN_W // 2, 0), GRID_W - WIN_W)
        for j in range(cs, cs + WIN_W):
            csel[c, j, j - c + WIN_W - 1] = 1.0
    return rsel, csel


def _na_bias_blocks(na_bias, rows, shifts):
    rsel, csel = _na_selectors(rows)
    depth = na_bias.shape[0]
    n_rel = 2 * WIN_H
    slab = jnp.einsum("lhde,cje->lhdjc", na_bias.astype(F32), csel, precision=lax.Precision.HIGHEST)
    slab = slab * LOG2E - shifts.reshape(-1, 1, 1, 1, 1)
    slab = jnp.where((csel.sum(-1) > 0.5).T, slab, NEG)
    slab = jnp.concatenate([slab, jnp.full_like(slab[:, :, :1], NEG)], axis=2)
    rel_row = np.where(rsel.sum(-1) > 0.5, rsel.argmax(-1), n_rel - 1).astype(np.int32)

    def assemble(rel_ref, slab_ref, o_ref):
        v, i = pl.program_id(1), pl.program_id(2)
        for t in range(NA_QROWS):
            d = rel_ref[(v * NA_QROWS + t) * NA_KROWS + i]
            o_ref[0, 0, :, :, GRID_W * t:GRID_W * (t + 1)] = slab_ref[0, :, d]

    return pl.pallas_call(
        assemble,
        grid_spec=pltpu.PrefetchScalarGridSpec(
            num_scalar_prefetch=1,
            grid=(depth, 3, NA_KROWS),
            in_specs=[pl.BlockSpec((1, N_HEADS_NA, n_rel, GRID_W, GRID_W), lambda l, v, i, rel: (l, 0, 0, 0, 0))],
            out_specs=pl.BlockSpec((1, 1, N_HEADS_NA, GRID_W, NA_QROWS * GRID_W), lambda l, v, i, rel: (l, v, 0, i, 0)),
        ),
        out_shape=jax.ShapeDtypeStruct((depth, 3, N_HEADS_NA, NA_KROWS * GRID_W, NA_QROWS * GRID_W), F32),
        compiler_params=_params(("parallel", "parallel", "parallel")),
        name="natt_bias_blocks",
    )(jnp.asarray(rel_row.reshape(-1)), slab)


def _na_kernel(qt_ref, k0_ref, k1_ref, k2_ref, v0_ref, v1_ref, v2_ref, bias_ref, o_ref, *, subtract_max):
    tq = qt_ref.shape[2]
    kwin = jnp.concatenate([k0_ref[0], k1_ref[0], k2_ref[0]], axis=0)
    vtwin = jnp.concatenate([v0_ref[0], v1_ref[0], v2_ref[0]], axis=1)
    row = lax.broadcasted_iota(jnp.int32, (128, tq), 0)

    def scores(h):
        return jnp.dot(kwin[:, 128 * (h // 2):128 * (h // 2) + 128], _masked_q_group(qt_ref, h, row),
                       preferred_element_type=F32) + bias_ref[0, h]

    outs = []
    s_next = scores(0)
    for h in range(N_HEADS_NA):
        s = s_next
        if h + 1 < N_HEADS_NA:
            s_next = scores(h + 1)
        if subtract_max:
            s = s - jnp.max(s, axis=0, keepdims=True)
        p = jnp.exp2(s)
        l = jnp.sum(p, axis=0, keepdims=True)
        vt = vtwin[HEAD_DIM * h:HEAD_DIM * (h + 1), :]
        outs.append(jnp.dot(vt, p.astype(BF16), preferred_element_type=F32) * (1.0 / l))
    o_ref[0] = jnp.concatenate(outs, axis=0).T.astype(o_ref.dtype)


def _na_call(subtract_max, operands):
    qnt, kn, vnt, bias_blocks = operands
    b, s, _ = kn.shape
    rows = s // GRID_W
    nblk = rows // NA_QROWS
    tq = NA_QROWS * GRID_W
    first = lambda j: jnp.clip(j - 1, 0, nblk - 3)
    variant = lambda i, j: (jnp.where(j == 0, 0, jnp.where(j == nblk - 1, 2, 1)), 0, 0, 0)
    kblk = lambda d: pl.BlockSpec((1, tq, W_NA), lambda i, j: (i, first(j) + d, 0))
    vblk = lambda d: pl.BlockSpec((1, W_NA, tq), lambda i, j: (i, 0, first(j) + d))
    return pl.pallas_call(
        functools.partial(_na_kernel, subtract_max=subtract_max),
        grid=(b, nblk),
        in_specs=[pl.BlockSpec((1, W_NA, tq), lambda i, j: (i, 0, j)),
                  kblk(0), kblk(1), kblk(2), vblk(0), vblk(1), vblk(2),
                  pl.BlockSpec((1, N_HEADS_NA, NA_KROWS * GRID_W, tq), variant)],
        out_specs=pl.BlockSpec((1, tq, W_NA), lambda i, j: (i, j, 0)),
        out_shape=jax.ShapeDtypeStruct((b, s, W_NA), BF16),
        compiler_params=_params(("parallel", "arbitrary")),
        name="natt_online" if subtract_max else "natt_static",
    )(qnt, kn, kn, kn, vnt, vnt, vnt, bias_blocks)


def _na_attention(shift_is_static, qnt, kn, vnt, bias_blocks):
    return lax.cond(shift_is_static, functools.partial(_na_call, False), functools.partial(_na_call, True),
                    (qnt, kn, vnt, bias_blocks))


def _merge_kernel(x_ref, oa_ref, on_ref, gmix_ref, wga_ref, wgb_ref, wpa_ref, wpb_ref, wo_ref, y_ref):
    x = x_ref[0]
    h = _rms(x, gmix_ref[...]).astype(BF16)
    ga = jnp.dot(h, wga_ref[...], preferred_element_type=F32)
    gb = jnp.dot(h, wgb_ref[...], preferred_element_type=F32)
    pa = jnp.dot(oa_ref[0], wpa_ref[...], preferred_element_type=F32)
    pb = jnp.dot(on_ref[0], wpb_ref[...], preferred_element_type=F32)
    mixed = jax.nn.sigmoid(ga) * pa + jax.nn.sigmoid(gb) * pb
    y_ref[0] = x + jnp.dot(mixed.astype(BF16), wo_ref[...], preferred_element_type=F32)


def _merge(x, oa, on, layer, gmix, w_in, w_pa, w_pb, w_o, tm):
    b, s, _ = x.shape
    tok = lambda i, j: (i, j, 0)
    gate_block = ATT_COLS // D_MODEL
    return pl.pallas_call(
        _merge_kernel,
        grid=(b, s // tm),
        in_specs=[
            pl.BlockSpec((1, tm, D_MODEL), tok),
            pl.BlockSpec((1, tm, W_DIFF), tok),
            pl.BlockSpec((1, tm, W_NA), tok),
            _const_spec((1, D_MODEL)),
            _const_spec((None, D_MODEL, D_MODEL), (layer, 0, gate_block)),
            _const_spec((None, D_MODEL, D_MODEL), (layer, 0, gate_block + 1)),
            _const_spec((None, W_DIFF, D_MODEL), (layer, 0, 0)),
            _const_spec((None, W_NA, D_MODEL), (layer, 0, 0)),
            _const_spec((None, D_MODEL, D_MODEL), (layer, 0, 0)),
        ],
        out_specs=pl.BlockSpec((1, tm, D_MODEL), tok),
        out_shape=jax.ShapeDtypeStruct(x.shape, F32),
        compiler_params=_params(("parallel", "parallel")),
        name="merge",
    )(x, oa, on, gmix, w_in, w_in, w_pa, w_pb, w_o)


def _ffn_kernel(x_ref, xp_ref, xn_ref, g_ref, wup_ref, cw_ref, cb_ref, wdn_ref, y_ref):
    j = pl.program_id(1)
    tm = x_ref.shape[1]
    x = x_ref[0]
    xe = jnp.concatenate([xp_ref[0], x, xn_ref[0]], axis=0)
    row = lax.broadcasted_iota(jnp.int32, (tm + 2 * HALO, 1), 0)
    inside = jnp.logical_and(jnp.logical_or(row >= HALO, j > 0),
                             jnp.logical_or(row < tm + HALO, j < pl.num_programs(1) - 1))
    he = jnp.where(inside, _rms(xe, g_ref[...]), 0.0).astype(BF16)

    def up(c0, n):
        return tuple(jnp.dot(he, wup_ref[:, c:c + n], preferred_element_type=F32)
                     for c in (c0, D_FF + c0))

    def conv(u, c0, n):
        w = cw_ref[:, c0:c0 + n]
        prev = pltpu.roll(u, 1, 0)[HALO:HALO + tm]
        nxt = pltpu.roll(u, tm + 2 * HALO - 1, 0)[HALO:HALO + tm]
        return prev * w[0:1] + u[HALO:HALO + tm] * w[1:2] + nxt * w[2:3] + cb_ref[:, c0:c0 + n]

    def gelu(v):
        c = math.sqrt(2.0 / math.pi)
        half_v = 0.5 * v
        return half_v + half_v * jnp.tanh(v * (c + (c * 0.044715) * (v * v)))

    starts = [sum(FFN_CHUNKS[:i]) for i in range(len(FFN_CHUNKS))]
    acc = x
    u_next = up(starts[0], FFN_CHUNKS[0])
    for i, (c0, n) in enumerate(zip(starts, FFN_CHUNKS)):
        ug, uv = u_next
        if i + 1 < len(FFN_CHUNKS):
            u_next = up(starts[i + 1], FFN_CHUNKS[i + 1])
        act = (gelu(conv(ug, c0, n)) * conv(uv, D_FF + c0, n)).astype(BF16)
        acc = acc + jnp.dot(act, wdn_ref[c0:c0 + n, :], preferred_element_type=F32)
    y_ref[0] = acc


def _ffn(x, layer, g, w_up, conv_w, conv_b, w_down, tm):
    b, s, _ = x.shape
    nb = tm // HALO
    last = s // HALO - 1
    return pl.pallas_call(
        _ffn_kernel,
        grid=(b, s // tm),
        in_specs=[
            pl.BlockSpec((1, tm, D_MODEL), lambda i, j: (i, j, 0)),
            pl.BlockSpec((1, HALO, D_MODEL), lambda i, j: (i, jnp.maximum(j * nb - 1, 0), 0)),
            pl.BlockSpec((1, HALO, D_MODEL), lambda i, j: (i, jnp.minimum((j + 1) * nb, last), 0)),
            _const_spec((1, D_MODEL)),
            _const_spec((None, D_MODEL, 2 * D_FF), (layer, 0, 0)),
            _const_spec((None, 3, 2 * D_FF), (layer, 0, 0)),
            _const_spec((None, 1, 2 * D_FF), (layer, 0, 0)),
            _const_spec((None, D_FF, D_MODEL), (layer, 0, 0)),
        ],
        out_specs=pl.BlockSpec((1, tm, D_MODEL), lambda i, j: (i, j, 0)),
        out_shape=jax.ShapeDtypeStruct(x.shape, F32),
        compiler_params=_params(("parallel", "parallel")),
        name="ffn",
    )(x, x, x, g, w_up, conv_w, conv_b, w_down)


def _rope_tables(seq):
    half = HEAD_DIM // 2
    inv = jnp.power(ROPE_THETA, -jnp.arange(half, dtype=F32) * 2.0 / HEAD_DIM)
    ang = jnp.arange(seq, dtype=F32)[:, None] * inv[None, :]
    cos, sin = jnp.cos(ang), jnp.sin(ang)
    return jnp.tile(cos, (1, 4)), jnp.tile(jnp.concatenate([-sin, sin], axis=-1), (1, 2)), cos.T, sin.T


def _group_sum_matrix():
    idx = np.arange(256) // HEAD_DIM
    return jnp.asarray(idx[:, None] == idx[None, :], dtype=BF16)


def _tiles(seq):
    tm = min(1024, seq)
    tm_ffn = min(512, seq)
    tq = min(1024, seq)
    tk = min(2048, seq)
    return tm, tm_ffn, tq, tk


def _trunk(x, weights, na_bias, na_shift):
    stacked, per_layer = weights
    seq = x.shape[1]
    tm, tm_ffn, tq, tk = _tiles(seq)
    cos, sin, cost, sint = _rope_tables(seq)
    gsum = _group_sum_matrix()
    na_static = na_shift <= MAX_STATIC_SHIFT
    bias_blocks = _na_bias_blocks(na_bias, seq // GRID_W, jnp.where(na_static, na_shift, 0.0))
    for l, w in enumerate(per_layer):
        lambda_init = 0.8 - 0.6 * math.exp(-0.3 * l)
        qt, k, vt, qnt, kn, vnt = _proj(x, l, w["g_mix"], stacked["w_in"], w["gq_a"], w["gk_a"], w["gq_n"],
                                        w["gk_n"], cos, sin, cost, sint, gsum, tm)
        oa = _diff_attention(w["shift"], qt, k, vt, w["lam_q1"], w["lam_k1"], w["lam_q2"], w["lam_k2"], w["g_sub"],
                             lambda_init, tq, tk)
        on = _na_attention(na_static[l], qnt, kn, vnt, bias_blocks[l])
        x = _merge(x, oa, on, l, w["g_mix"], stacked["w_in"], stacked["w_pa"], stacked["w_pb"], stacked["w_o"], tm)
        x = _ffn(x, l, w["g_ffn"], stacked["w_up"], stacked["conv_w"], stacked["conv_b"], stacked["w_down"],
                 tm_ffn)
    return x


def _score_bound(gq, gk):
    bound = (HEAD_DIM ** 0.5) * LOG2E * jnp.max(jnp.abs(gq)) * jnp.max(jnp.abs(gk))
    return jnp.ceil(1.02 * bound.astype(F32)).reshape(1)


def _na_score_bound(gq, gk, na_bias):
    amax = lambda a: jnp.max(jnp.abs(a.astype(F32)).reshape(a.shape[0], -1), axis=1)
    bound = LOG2E * ((HEAD_DIM ** 0.5) * amax(gq) * amax(gk) + amax(na_bias))
    return jnp.ceil(1.02 * bound)


def _layer_weights(g_mix, w_in, gq_a, gk_a, lam_q1, lam_k1, lam_q2, lam_k2, g_sub, gq_n, gk_n,
                   w_pa, w_pb, w_o, g_ffn, w_up, conv_w, conv_b, w_down):
    depth = w_in.shape[0]
    stacked = dict(w_in=w_in.astype(BF16), w_pa=w_pa.astype(BF16), w_pb=w_pb.astype(BF16), w_o=w_o.astype(BF16),
                   w_up=w_up.astype(BF16), w_down=w_down.astype(BF16), conv_w=conv_w.astype(F32),
                   conv_b=conv_b.astype(F32)[:, None, :])
    head_tile = lambda g: jnp.tile(g.astype(F32), W_DIFF // HEAD_DIM)[None, :]
    row = lambda g: g.astype(F32)[None, :]
    col = lambda g: g.astype(F32)[:, None]
    per_layer = []
    for l in range(depth):
        per_layer.append(dict(
            g_mix=row(g_mix[l]), gq_a=col(gq_a[l]), gk_a=head_tile(gk_a[l]), gq_n=col(gq_n[l]),
            gk_n=head_tile(gk_n[l]),
            lam_q1=row(lam_q1[l]), lam_k1=row(lam_k1[l]), lam_q2=row(lam_q2[l]), lam_k2=row(lam_k2[l]),
            g_sub=col(g_sub[l]), shift=_score_bound(gq_a[l], gk_a[l]), g_ffn=row(g_ffn[l])))
    return stacked, per_layer


def kernel(x_prompt, x_sample, g_mix, w_in, gq_a, gk_a, lam_q1, lam_k1, lam_q2, lam_k2, g_sub, gq_n, gk_n,
           na_bias, w_pa, w_pb, w_o, g_ffn, w_up, conv_w, conv_b, w_down):
    weights = _layer_weights(g_mix, w_in, gq_a, gk_a, lam_q1, lam_k1, lam_q2, lam_k2, g_sub, gq_n, gk_n,
                             w_pa, w_pb, w_o, g_ffn, w_up, conv_w, conv_b, w_down)
    na_shift = _na_score_bound(gq_n, gk_n, na_bias)
    return tuple(_trunk(x, weights, na_bias, na_shift) for x in (x_prompt, x_sample))
```

```python
import functools
import math

import numpy as np
import jax
import jax.numpy as jnp
from jax import lax
from jax.experimental import pallas as pl
from jax.experimental.pallas import tpu as pltpu

D_MODEL = 1024
HEAD_DIM = 64
N_HEADS_DIFF = 4
N_HEADS_NA = 8
W_DIFF = N_HEADS_DIFF * 2 * HEAD_DIM
W_NA = N_HEADS_NA * HEAD_DIM
ATT_COLS = 3 * W_DIFF + 3 * W_NA
GRID_W = 64
WIN_H = 8
WIN_W = 16
D_FF = 2816
ROPE_THETA = 10000.0
EPS = 1e-6
NEG = -1e30

NA_QROWS = 4
NA_KROWS = 12
FFN_CHUNKS = (2304, 512)
HALO = 8

VMEM_LIMIT = 56 * 1024 * 1024

F32 = jnp.float32
BF16 = jnp.bfloat16


def _const_spec(shape, index=None):
    index = (0,) * len(shape) if index is None else tuple(index)
    return pl.BlockSpec(shape, lambda *_: index, pipeline_mode=pl.Buffered(1))


def _params(sem):
    return pltpu.CompilerParams(dimension_semantics=sem, vmem_limit_bytes=VMEM_LIMIT)


def _rms(x, g):
    return x * lax.rsqrt(jnp.mean(x * x, axis=-1, keepdims=True) + EPS) * g


def _proj_kernel(x_ref, gmix_ref, w_ref, gqa_ref, gka_ref, gqn_ref, gkn_ref, cos_ref, sin_ref, cost_ref, sint_ref,
                 gsum_ref, qt_ref, k_ref, vt_ref, qnt_ref, kn_ref, vnt_ref):
    x = x_ref[0]
    h = _rms(x, gmix_ref[...]).astype(BF16)
    gsum = gsum_ref[...]
    half = HEAD_DIM // 2

    def proj(c0):
        return jnp.dot(h, w_ref[:, c0:c0 + W_DIFF], preferred_element_type=F32)

    def head_norm_t(zt, gcol):
        heads = []
        for r in range(0, W_DIFF, HEAD_DIM):
            blk = zt[r:r + HEAD_DIM]
            heads.append(blk * lax.rsqrt(jnp.mean(blk * blk, axis=0, keepdims=True) + EPS) * gcol)
        return heads

    def rope_t(blk):
        x1, x2 = blk[:half], blk[half:]
        return jnp.concatenate([x1 * cost_ref[...] - x2 * sint_ref[...], x1 * sint_ref[...] + x2 * cost_ref[...]],
                               axis=0)

    def head_norm(z, g):
        z2 = z * z
        hi = z2.astype(BF16)
        lo = (z2 - hi.astype(F32)).astype(BF16)
        parts = []
        for c in range(0, W_DIFF, 256):
            parts.append(jnp.dot(hi[:, c:c + 256], gsum, preferred_element_type=F32)
                         + jnp.dot(lo[:, c:c + 256], gsum, preferred_element_type=F32))
        ss = jnp.concatenate(parts, axis=-1)
        return z * lax.rsqrt(ss * (1.0 / HEAD_DIM) + EPS) * g

    cos = jnp.concatenate([cos_ref[...]] * 4, axis=-1)
    sin = jnp.concatenate([sin_ref[...]] * 4, axis=-1)
    lane = lax.broadcasted_iota(jnp.int32, (1, W_DIFF), 1)
    first_half = (lane % HEAD_DIM) < (HEAD_DIM // 2)

    def rope(y):
        partner = jnp.where(first_half, pltpu.roll(y, W_DIFF - HEAD_DIM // 2, 1), pltpu.roll(y, HEAD_DIM // 2, 1))
        return y * cos + partner * sin

    scale = HEAD_DIM ** -0.5 * LOG2E

    def emit_qa(z):
        qa = [rope_t(blk) for blk in head_norm_t(z.T, gqa_ref[...])]
        qt_ref[0] = (jnp.concatenate(qa, axis=0) * scale).astype(BF16)

    def emit_ka(z):
        k_ref[0] = rope(head_norm(z, gka_ref[...])).astype(BF16)

    def emit_va(z):
        vt_ref[0] = z.T.astype(BF16)

    def emit_qn(z):
        qnt_ref[0] = (jnp.concatenate(head_norm_t(z.T, gqn_ref[...]), axis=0) * scale).astype(BF16)

    def emit_kn(z):
        kn_ref[0] = head_norm(z, gkn_ref[...]).astype(BF16)

    def emit_vn(z):
        vnt_ref[0] = z.T.astype(BF16)

    emitters = (emit_qa, emit_ka, emit_va, emit_qn, emit_kn, emit_vn)
    z_next = proj(0)
    for i, emit in enumerate(emitters):
        z = z_next
        if i + 1 < len(emitters):
            z_next = proj((i + 1) * W_DIFF)
        emit(z)


def _proj(x, layer, gmix, w_in, gqa, gka, gqn, gkn, cos, sin, cost, sint, gsum, tm):
    b, s, _ = x.shape
    tok = lambda i, j: (i, j, 0)
    tr = lambda i, j: (i, 0, j)
    row_major = jax.ShapeDtypeStruct((b, s, W_DIFF), BF16)
    col_major = jax.ShapeDtypeStruct((b, W_DIFF, s), BF16)
    half = HEAD_DIM // 2
    return pl.pallas_call(
        _proj_kernel,
        grid=(b, s // tm),
        in_specs=[
            pl.BlockSpec((1, tm, D_MODEL), tok),
            _const_spec((1, D_MODEL)),
            _const_spec((None, D_MODEL, ATT_COLS), (layer, 0, 0)),
            _const_spec((HEAD_DIM, 1)), _const_spec((1, W_DIFF)), _const_spec((HEAD_DIM, 1)), _const_spec((1, W_NA)),
            pl.BlockSpec((tm, 128), lambda i, j: (j, 0)),
            pl.BlockSpec((tm, 128), lambda i, j: (j, 0)),
            pl.BlockSpec((half, tm), lambda i, j: (0, j)),
            pl.BlockSpec((half, tm), lambda i, j: (0, j)),
            _const_spec((256, 256)),
        ],
        out_specs=[
            pl.BlockSpec((1, W_DIFF, tm), tr),
            pl.BlockSpec((1, tm, W_DIFF), tok),
            pl.BlockSpec((1, W_DIFF, tm), tr),
            pl.BlockSpec((1, W_NA, tm), tr),
            pl.BlockSpec((1, tm, W_NA), tok),
            pl.BlockSpec((1, W_NA, tm), tr),
        ],
        out_shape=[col_major, row_major, col_major, col_major, row_major, col_major],
        compiler_params=_params(("parallel", "parallel")),
        name="proj",
    )(x, gmix, w_in, gqa, gka, gqn, gkn, cos, sin, cost, sint, gsum)


N_MAPS = 2 * N_HEADS_DIFF
LOG2E = math.log2(math.e)
MAX_STATIC_SHIFT = 60.0
DIFF_KEY_CHUNK = 2048


def _masked_q_group(qt_ref, j, row):
    grp = qt_ref[0, 128 * (j // 2):128 * (j // 2) + 128, :]
    keep = (row >= HEAD_DIM) if (j % 2) else (row < HEAD_DIM)
    return jnp.where(keep, grp, jnp.zeros_like(grp))


def _diff_finish(acc_sc, l_of, lq1_ref, lk1_ref, lq2_ref, lk2_ref, gsub_ref, o_ref, lambda_init):
    lam = (jnp.exp(jnp.sum(lq1_ref[...] * lk1_ref[...])) - jnp.exp(jnp.sum(lq2_ref[...] * lk2_ref[...]))
           + lambda_init)
    for h in range(N_HEADS_DIFF):
        o0 = acc_sc[h] * (1.0 / l_of(h))
        o1 = acc_sc[N_HEADS_DIFF + h] * (lam / l_of(N_HEADS_DIFF + h))
        o = o0 - o1
        o = o * (lax.rsqrt(jnp.mean(o * o, axis=0, keepdims=True) + EPS) * (1.0 - lambda_init)) * gsub_ref[...]
        o_ref[0, :, 128 * h:128 * h + 128] = o.T.astype(o_ref.dtype)


def _diff_online_kernel(shift_ref, qt_ref, k_ref, vt_ref, lq1_ref, lk1_ref, lq2_ref, lk2_ref, gsub_ref, o_ref,
                        qz_sc, m_sc, l_sc, acc_sc, *, lambda_init):
    del shift_ref
    kv = pl.program_id(2)
    tq = qt_ref.shape[2]

    @pl.when(kv == 0)
    def _init():
        m_sc[...] = jnp.full(m_sc.shape, NEG, F32)
        l_sc[...] = jnp.zeros(l_sc.shape, F32)
        acc_sc[...] = jnp.zeros(acc_sc.shape, F32)
        row = lax.broadcasted_iota(jnp.int32, (128, tq), 0)
        for j in range(N_MAPS):
            qz_sc[j] = _masked_q_group(qt_ref, j, row)

    for j in range(N_MAPS):
        h = j % N_HEADS_DIFF
        kp = k_ref[0, :, 128 * (j // 2):128 * (j // 2) + 128]
        s = jnp.dot(kp, qz_sc[j], preferred_element_type=F32)
        m_prev = m_sc[j]
        m_cur = jnp.maximum(m_prev, jnp.max(s, axis=0, keepdims=True))
        alpha = jnp.exp2(m_prev - m_cur)
        p = jnp.exp2(s - m_cur)
        l_sc[j] = alpha * l_sc[j] + jnp.sum(p, axis=0, keepdims=True)
        vt = vt_ref[0, 128 * h:128 * h + 128, :]
        acc_sc[j] = alpha * acc_sc[j] + jnp.dot(vt, p.astype(BF16), preferred_element_type=F32)
        m_sc[j] = m_cur

    @pl.when(kv == pl.num_programs(2) - 1)
    def _finish():
        _diff_finish(acc_sc, lambda j: l_sc[j], lq1_ref, lk1_ref, lq2_ref, lk2_ref, gsub_ref, o_ref, lambda_init)


def _diff_static_kernel(shift_ref, qt_ref, k_ref, vt_ref, lq1_ref, lk1_ref, lq2_ref, lk2_ref, gsub_ref, o_ref,
                        qz_sc, l_sc, acc_sc, *, lambda_init):
    kv = pl.program_id(2)
    tq = qt_ref.shape[2]
    tk = k_ref.shape[1]

    @pl.when(kv == 0)
    def _init():
        l_sc[...] = jnp.zeros(l_sc.shape, F32)
        acc_sc[...] = jnp.zeros(acc_sc.shape, F32)
        row = lax.broadcasted_iota(jnp.int32, (128, tq), 0)
        shift_rows = jnp.where(row == 0, -shift_ref[0], 0.0).astype(BF16)
        for j in range(N_MAPS):
            qz_sc[j, :128, :] = _masked_q_group(qt_ref, j, row)
            qz_sc[j, 128:, :] = shift_rows

    tc = min(DIFF_KEY_CHUNK, tk)
    ones = jnp.ones((tc, 128), BF16)
    units = [(j, c) for j in range(N_MAPS) for c in range(0, tk, tc)]

    def scores(j, c):
        kaug = jnp.concatenate([k_ref[0, c:c + tc, 128 * (j // 2):128 * (j // 2) + 128], ones], axis=1)
        return jnp.dot(kaug, qz_sc[j], preferred_element_type=F32)

    s_next = scores(*units[0])
    for u, (j, c) in enumerate(units):
        h = j % N_HEADS_DIFF
        s = s_next
        if u + 1 < len(units):
            s_next = scores(*units[u + 1])
        p = jnp.exp2(s)
        l_sc[j] += jnp.sum(p.reshape(tc // 8, 8, tq), axis=0)
        vt = vt_ref[0, 128 * h:128 * h + 128, c:c + tc]
        acc_sc[j] += jnp.dot(vt, p.astype(BF16), preferred_element_type=F32)

    @pl.when(kv == pl.num_programs(2) - 1)
    def _finish():
        _diff_finish(acc_sc, lambda j: jnp.sum(l_sc[j], axis=0, keepdims=True),
                     lq1_ref, lk1_ref, lq2_ref, lk2_ref, gsub_ref, o_ref, lambda_init)


def _diff_call(body, scratch, operands, lambda_init, tq, tk):
    qt = operands[1]
    b, _, s = qt.shape
    return pl.pallas_call(
        functools.partial(body, lambda_init=lambda_init),
        grid=(b, s // tq, s // tk),
        in_specs=[
            pl.BlockSpec(memory_space=pltpu.SMEM),
            pl.BlockSpec((1, W_DIFF, tq), lambda i, j, t: (i, 0, j)),
            pl.BlockSpec((1, tk, W_DIFF), lambda i, j, t: (i, t, 0)),
            pl.BlockSpec((1, W_DIFF, tk), lambda i, j, t: (i, 0, t)),
            _const_spec((1, HEAD_DIM)), _const_spec((1, HEAD_DIM)), _const_spec((1, HEAD_DIM)),
            _const_spec((1, HEAD_DIM)),
            _const_spec((2 * HEAD_DIM, 1)),
        ],
        out_specs=pl.BlockSpec((1, tq, W_DIFF), lambda i, j, t: (i, j, 0)),
        out_shape=jax.ShapeDtypeStruct((b, s, W_DIFF), BF16),
        scratch_shapes=scratch,
        compiler_params=_params(("parallel", "parallel", "arbitrary")),
        name=body.__name__.strip("_"),
    )(*operands)


def _diff_attention(shift, qt, k, vt, lq1, lk1, lq2, lk2, gsub, lambda_init, tq, tk):
    operands = (shift, qt, k, vt, lq1, lk1, lq2, lk2, gsub)
    static = functools.partial(
        _diff_call, _diff_static_kernel,
        [pltpu.VMEM((N_MAPS, 256, tq), BF16), pltpu.VMEM((N_MAPS, 8, tq), F32), pltpu.VMEM((N_MAPS, 128, tq), F32)],
        lambda_init=lambda_init, tq=tq, tk=tk)
    online = functools.partial(
        _diff_call, _diff_online_kernel,
        [pltpu.VMEM((N_MAPS, 128, tq), BF16), pltpu.VMEM((N_MAPS, 1, tq), F32), pltpu.VMEM((N_MAPS, 1, tq), F32),
         pltpu.VMEM((N_MAPS, 128, tq), F32)],
        lambda_init=lambda_init, tq=tq, tk=tk)
    return lax.cond(shift[0] <= MAX_STATIC_SHIFT, static, online, operands)


def _na_selectors(rows):
    kh = min(WIN_H, rows)
    assert kh == WIN_H and rows >= NA_KROWS + NA_QROWS
    rsel = np.zeros((3, NA_QROWS, NA_KROWS, 2 * WIN_H - 1), np.float32)
    for v in range(3):
        r0 = (0, NA_QROWS, rows - NA_QROWS)[v]
        ws = min(max(r0 - 4, 0), rows - NA_KROWS)
        for t in range(NA_QROWS):
            r = r0 + t
            rs = min(max(r - kh // 2, 0), rows - kh)
            for i in range(NA_KROWS):
                kr = ws + i
                if rs <= kr < rs + kh:
                    rsel[v, t, i, kr - r + WIN_H - 1] = 1.0
    csel = np.zeros((GRID_W, GRID_W, 2 * WIN_W - 1), np.float32)
    for c in range(GRID_W):
        cs = min(max(c - WIN_W // 2, 0), GRID_W - WIN_W)
        for j in range(cs, cs + WIN_W):
            csel[c, j, j - c + WIN_W - 1] = 1.0
    return rsel, csel


def _na_bias_blocks(na_bias, rows, shifts):
    rsel, csel = _na_selectors(rows)
    depth = na_bias.shape[0]
    n_rel = 2 * WIN_H
    slab = jnp.einsum("lhde,cje->lhdjc", na_bias.astype(F32), csel, precision=lax.Precision.HIGHEST)
    slab = slab * LOG2E - shifts.reshape(-1, 1, 1, 1, 1)
    slab = jnp.where((csel.sum(-1) > 0.5).T, slab, NEG)
    slab = jnp.concatenate([slab, jnp.full_like(slab[:, :, :1], NEG)], axis=2)
    rel_row = np.where(rsel.sum(-1) > 0.5, rsel.argmax(-1), n_rel - 1).astype(np.int32)

    def assemble(rel_ref, slab_ref, o_ref):
        v, i = pl.program_id(1), pl.program_id(2)
        for t in range(NA_QROWS):
            d = rel_ref[(v * NA_QROWS + t) * NA_KROWS + i]
            o_ref[0, 0, :, :, GRID_W * t:GRID_W * (t + 1)] = slab_ref[0, :, d]

    return pl.pallas_call(
        assemble,
        grid_spec=pltpu.PrefetchScalarGridSpec(
            num_scalar_prefetch=1,
            grid=(depth, 3, NA_KROWS),
            in_specs=[pl.BlockSpec((1, N_HEADS_NA, n_rel, GRID_W, GRID_W), lambda l, v, i, rel: (l, 0, 0, 0, 0))],
            out_specs=pl.BlockSpec((1, 1, N_HEADS_NA, GRID_W, NA_QROWS * GRID_W), lambda l, v, i, rel: (l, v, 0, i, 0)),
        ),
        out_shape=jax.ShapeDtypeStruct((depth, 3, N_HEADS_NA, NA_KROWS * GRID_W, NA_QROWS * GRID_W), F32),
        compiler_params=_params(("parallel", "parallel", "parallel")),
        name="natt_bias_blocks",
    )(jnp.asarray(rel_row.reshape(-1)), slab)


def _na_kernel(qt_ref, k_ref, vt_ref, bias_ref, o_ref, *, subtract_max):
    tq = qt_ref.shape[2]
    kwin = k_ref[0]
    vtwin = vt_ref[0]
    row = lax.broadcasted_iota(jnp.int32, (128, tq), 0)

    def scores(h):
        return jnp.dot(kwin[:, 128 * (h // 2):128 * (h // 2) + 128], _masked_q_group(qt_ref, h, row),
                       preferred_element_type=F32) + bias_ref[0, h]

    outs = []
    s_next = scores(0)
    for h in range(N_HEADS_NA):
        s = s_next
        if h + 1 < N_HEADS_NA:
            s_next = scores(h + 1)
        if subtract_max:
            s = s - jnp.max(s, axis=0, keepdims=True)
        p = jnp.exp2(s)
        l = jnp.sum(p, axis=0, keepdims=True)
        vt = vtwin[HEAD_DIM * h:HEAD_DIM * (h + 1), :]
        outs.append(jnp.dot(vt, p.astype(BF16), preferred_element_type=F32) * (1.0 / l))
    o_ref[0] = jnp.concatenate(outs, axis=0).T.astype(o_ref.dtype)


def _na_call(subtract_max, layer, operands):
    qnt, kn, vnt, bias_blocks = operands
    b, s, _ = kn.shape
    rows = s // GRID_W
    nblk = rows // NA_QROWS
    tq = NA_QROWS * GRID_W
    win = NA_KROWS * GRID_W
    first = lambda j: jnp.clip(j - 1, 0, nblk - 3) * tq
    variant = lambda i, j: (layer, jnp.where(j == 0, 0, jnp.where(j == nblk - 1, 2, 1)), 0, 0, 0)
    kblk = pl.BlockSpec((pl.Element(1), pl.Element(win), pl.Element(W_NA)), lambda i, j: (i, first(j), 0))
    vblk = pl.BlockSpec((pl.Element(1), pl.Element(W_NA), pl.Element(win)), lambda i, j: (i, 0, first(j)))
    return pl.pallas_call(
        functools.partial(_na_kernel, subtract_max=subtract_max),
        grid=(b, nblk),
        in_specs=[pl.BlockSpec((1, W_NA, tq), lambda i, j: (i, 0, j)),
                  kblk, vblk,
                  pl.BlockSpec((None, 1, N_HEADS_NA, win, tq), variant)],
        out_specs=pl.BlockSpec((1, tq, W_NA), lambda i, j: (i, j, 0)),
        out_shape=jax.ShapeDtypeStruct((b, s, W_NA), BF16),
        compiler_params=_params(("parallel", "arbitrary")),
        name="natt_online" if subtract_max else "natt_static",
    )(qnt, kn, vnt, bias_blocks)


def _na_attention(shift_is_static, layer, qnt, kn, vnt, bias_blocks):
    return lax.cond(shift_is_static, functools.partial(_na_call, False, layer),
                    functools.partial(_na_call, True, layer), (qnt, kn, vnt, bias_blocks))


def _merge_kernel(x_ref, oa_ref, on_ref, gmix_ref, wga_ref, wgb_ref, wpa_ref, wpb_ref, wo_ref, y_ref):
    x = x_ref[0]
    h = _rms(x, gmix_ref[...]).astype(BF16)
    ga = jnp.dot(h, wga_ref[...], preferred_element_type=F32)
    gb = jnp.dot(h, wgb_ref[...], preferred_element_type=F32)
    pa = jnp.dot(oa_ref[0], wpa_ref[...], preferred_element_type=F32)
    pb = jnp.dot(on_ref[0], wpb_ref[...], preferred_element_type=F32)
    mixed = jax.nn.sigmoid(ga) * pa + jax.nn.sigmoid(gb) * pb
    y_ref[0] = x + jnp.dot(mixed.astype(BF16), wo_ref[...], preferred_element_type=F32)


def _merge(x, oa, on, layer, gmix, w_in, w_pa, w_pb, w_o, tm):
    b, s, _ = x.shape
    tok = lambda i, j: (i, j, 0)
    gate_block = ATT_COLS // D_MODEL
    return pl.pallas_call(
        _merge_kernel,
        grid=(b, s // tm),
        in_specs=[
            pl.BlockSpec((1, tm, D_MODEL), tok),
            pl.BlockSpec((1, tm, W_DIFF), tok),
            pl.BlockSpec((1, tm, W_NA), tok),
            _const_spec((1, D_MODEL)),
            _const_spec((None, D_MODEL, D_MODEL), (layer, 0, gate_block)),
            _const_spec((None, D_MODEL, D_MODEL), (layer, 0, gate_block + 1)),
            _const_spec((None, W_DIFF, D_MODEL), (layer, 0, 0)),
            _const_spec((None, W_NA, D_MODEL), (layer, 0, 0)),
            _const_spec((None, D_MODEL, D_MODEL), (layer, 0, 0)),
        ],
        out_specs=pl.BlockSpec((1, tm, D_MODEL), tok),
        out_shape=jax.ShapeDtypeStruct(x.shape, F32),
        compiler_params=_params(("parallel", "parallel")),
        name="merge",
    )(x, oa, on, gmix, w_in, w_in, w_pa, w_pb, w_o)


def _ffn_kernel(x_ref, xp_ref, xn_ref, g_ref, wup_ref, cw_ref, cb_ref, wdn_ref, y_ref):
    j = pl.program_id(1)
    tm = x_ref.shape[1]
    x = x_ref[0]
    xe = jnp.concatenate([xp_ref[0], x, xn_ref[0]], axis=0)
    row = lax.broadcasted_iota(jnp.int32, (tm + 2 * HALO, 1), 0)
    inside = jnp.logical_and(jnp.logical_or(row >= HALO, j > 0),
                             jnp.logical_or(row < tm + HALO, j < pl.num_programs(1) - 1))
    he = jnp.where(inside, _rms(xe, g_ref[...]), 0.0).astype(BF16)

    def up(c0, n):
        return tuple(jnp.dot(he, wup_ref[:, c:c + n], preferred_element_type=F32)
                     for c in (c0, D_FF + c0))

    def conv(u, c0, n):
        w = cw_ref[:, c0:c0 + n]
        prev = pltpu.roll(u, 1, 0)[HALO:HALO + tm]
        nxt = pltpu.roll(u, tm + 2 * HALO - 1, 0)[HALO:HALO + tm]
        return prev * w[0:1] + u[HALO:HALO + tm] * w[1:2] + nxt * w[2:3] + cb_ref[:, c0:c0 + n]

    def gelu(v):
        c = math.sqrt(2.0 / math.pi)
        half_v = 0.5 * v
        return half_v + half_v * jnp.tanh(v * (c + (c * 0.044715) * (v * v)))

    starts = [sum(FFN_CHUNKS[:i]) for i in range(len(FFN_CHUNKS))]
    acc = x
    u_next = up(starts[0], FFN_CHUNKS[0])
    for i, (c0, n) in enumerate(zip(starts, FFN_CHUNKS)):
        ug, uv = u_next
        if i + 1 < len(FFN_CHUNKS):
            u_next = up(starts[i + 1], FFN_CHUNKS[i + 1])
        act = (gelu(conv(ug, c0, n)) * conv(uv, D_FF + c0, n)).astype(BF16)
        acc = acc + jnp.dot(act, wdn_ref[c0:c0 + n, :], preferred_element_type=F32)
    y_ref[0] = acc


def _ffn(x, layer, g, w_up, conv_w, conv_b, w_down, tm):
    b, s, _ = x.shape
    nb = tm // HALO
    last = s // HALO - 1
    return pl.pallas_call(
        _ffn_kernel,
        grid=(b, s // tm),
        in_specs=[
            pl.BlockSpec((1, tm, D_MODEL), lambda i, j: (i, j, 0)),
            pl.BlockSpec((1, HALO, D_MODEL), lambda i, j: (i, jnp.maximum(j * nb - 1, 0), 0)),
            pl.BlockSpec((1, HALO, D_MODEL), lambda i, j: (i, jnp.minimum((j + 1) * nb, last), 0)),
            _const_spec((1, D_MODEL)),
            _const_spec((None, D_MODEL, 2 * D_FF), (layer, 0, 0)),
            _const_spec((None, 3, 2 * D_FF), (layer, 0, 0)),
            _const_spec((None, 1, 2 * D_FF), (layer, 0, 0)),
            _const_spec((None, D_FF, D_MODEL), (layer, 0, 0)),
        ],
        out_specs=pl.BlockSpec((1, tm, D_MODEL), lambda i, j: (i, j, 0)),
        out_shape=jax.ShapeDtypeStruct(x.shape, F32),
        compiler_params=_params(("parallel", "parallel")),
        name="ffn",
    )(x, x, x, g, w_up, conv_w, conv_b, w_down)


def _rope_tables(seq):
    half = HEAD_DIM // 2
    inv = jnp.power(ROPE_THETA, -jnp.arange(half, dtype=F32) * 2.0 / HEAD_DIM)
    ang = jnp.arange(seq, dtype=F32)[:, None] * inv[None, :]
    cos, sin = jnp.cos(ang), jnp.sin(ang)
    return jnp.tile(cos, (1, 4)), jnp.tile(jnp.concatenate([-sin, sin], axis=-1), (1, 2)), cos.T, sin.T


def _group_sum_matrix():
    idx = np.arange(256) // HEAD_DIM
    return jnp.asarray(idx[:, None] == idx[None, :], dtype=BF16)


def _tiles(seq):
    tm = min(1024, seq)
    tm_ffn = min(512, seq)
    tq = min(1024, seq)
    tk = min(2048, seq)
    return tm, tm_ffn, tq, tk


def _trunk(x, weights, na_bias, na_shift):
    stacked, per_layer = weights
    seq = x.shape[1]
    tm, tm_ffn, tq, tk = _tiles(seq)
    cos, sin, cost, sint = _rope_tables(seq)
    gsum = _group_sum_matrix()
    na_static = na_shift <= MAX_STATIC_SHIFT
    bias_blocks = _na_bias_blocks(na_bias, seq // GRID_W, jnp.where(na_static, na_shift, 0.0))
    for l, w in enumerate(per_layer):
        lambda_init = 0.8 - 0.6 * math.exp(-0.3 * l)
        qt, k, vt, qnt, kn, vnt = _proj(x, l, w["g_mix"], stacked["w_in"], w["gq_a"], w["gk_a"], w["gq_n"],
                                        w["gk_n"], cos, sin, cost, sint, gsum, tm)
        oa = _diff_attention(w["shift"], qt, k, vt, w["lam_q1"], w["lam_k1"], w["lam_q2"], w["lam_k2"], w["g_sub"],
                             lambda_init, tq, tk)
        on = _na_attention(na_static[l], l, qnt, kn, vnt, bias_blocks)
        x = _merge(x, oa, on, l, w["g_mix"], stacked["w_in"], stacked["w_pa"], stacked["w_pb"], stacked["w_o"], tm)
        x = _ffn(x, l, w["g_ffn"], stacked["w_up"], stacked["conv_w"], stacked["conv_b"], stacked["w_down"],
                 tm_ffn)
    return x


def _score_bound(gq, gk):
    bound = (HEAD_DIM ** 0.5) * LOG2E * jnp.max(jnp.abs(gq)) * jnp.max(jnp.abs(gk))
    return jnp.ceil(1.02 * bound.astype(F32)).reshape(1)


def _na_score_bound(gq, gk, na_bias):
    amax = lambda a: jnp.max(jnp.abs(a.astype(F32)).reshape(a.shape[0], -1), axis=1)
    bound = LOG2E * ((HEAD_DIM ** 0.5) * amax(gq) * amax(gk) + amax(na_bias))
    return jnp.ceil(1.02 * bound)


def _layer_weights(g_mix, w_in, gq_a, gk_a, lam_q1, lam_k1, lam_q2, lam_k2, g_sub, gq_n, gk_n,
                   w_pa, w_pb, w_o, g_ffn, w_up, conv_w, conv_b, w_down):
    depth = w_in.shape[0]
    stacked = dict(w_in=w_in.astype(BF16), w_pa=w_pa.astype(BF16), w_pb=w_pb.astype(BF16), w_o=w_o.astype(BF16),
                   w_up=w_up.astype(BF16), w_down=w_down.astype(BF16), conv_w=conv_w.astype(F32),
                   conv_b=conv_b.astype(F32)[:, None, :])
    head_tile = lambda g: jnp.tile(g.astype(F32), W_DIFF // HEAD_DIM)[None, :]
    row = lambda g: g.astype(F32)[None, :]
    col = lambda g: g.astype(F32)[:, None]
    per_layer = []
    for l in range(depth):
        per_layer.append(dict(
            g_mix=row(g_mix[l]), gq_a=col(gq_a[l]), gk_a=head_tile(gk_a[l]), gq_n=col(gq_n[l]),
            gk_n=head_tile(gk_n[l]),
            lam_q1=row(lam_q1[l]), lam_k1=row(lam_k1[l]), lam_q2=row(lam_q2[l]), lam_k2=row(lam_k2[l]),
            g_sub=col(g_sub[l]), shift=_score_bound(gq_a[l], gk_a[l]), g_ffn=row(g_ffn[l])))
    return stacked, per_layer


def kernel(x_prompt, x_sample, g_mix, w_in, gq_a, gk_a, lam_q1, lam_k1, lam_q2, lam_k2, g_sub, gq_n, gk_n,
           na_bias, w_pa, w_pb, w_o, g_ffn, w_up, conv_w, conv_b, w_down):
    weights = _layer_weights(g_mix, w_in, gq_a, gk_a, lam_q1, lam_k1, lam_q2, lam_k2, g_sub, gq_n, gk_n,
                             w_pa, w_pb, w_o, g_ffn, w_up, conv_w, conv_b, w_down)
    na_shift = _na_score_bound(gq_n, gk_n, na_bias)
    return tuple(_trunk(x, weights, na_bias, na_shift) for x in (x_prompt, x_sample))
```

```python
import functools
import math

import numpy as np
import jax
import jax.numpy as jnp
from jax import lax
from jax.experimental import pallas as pl
from jax.experimental.pallas import tpu as pltpu

D_MODEL = 1024
HEAD_DIM = 64
N_HEADS_DIFF = 4
N_HEADS_NA = 8
W_DIFF = N_HEADS_DIFF * 2 * HEAD_DIM
W_NA = N_HEADS_NA * HEAD_DIM
ATT_COLS = 3 * W_DIFF + 3 * W_NA
GRID_W = 64
WIN_H = 8
WIN_W = 16
D_FF = 2816
ROPE_THETA = 10000.0
EPS = 1e-6
NEG = -1e30

NA_QROWS = 4
NA_KROWS = 12
NA_SUBBLOCKS = 4
FFN_CHUNKS = (2304, 512)
HALO = 8

VMEM_LIMIT = 56 * 1024 * 1024

F32 = jnp.float32
BF16 = jnp.bfloat16


def _const_spec(shape, index=None):
    index = (0,) * len(shape) if index is None else tuple(index)
    return pl.BlockSpec(shape, lambda *_: index, pipeline_mode=pl.Buffered(1))


def _params(sem):
    return pltpu.CompilerParams(dimension_semantics=sem, vmem_limit_bytes=VMEM_LIMIT)


def _rms(x, g):
    return x * lax.rsqrt(jnp.mean(x * x, axis=-1, keepdims=True) + EPS) * g


def _proj_kernel(x_ref, gmix_ref, w_ref, gqa_ref, gka_ref, gqn_ref, gkn_ref, cos_ref, sin_ref, cost_ref, sint_ref,
                 gsum_ref, qt_ref, k_ref, vt_ref, qnt_ref, kn_ref, vnt_ref):
    x = x_ref[0]
    h = _rms(x, gmix_ref[...]).astype(BF16)
    gsum = gsum_ref[...]
    half = HEAD_DIM // 2

    def proj(c0):
        return jnp.dot(h, w_ref[:, c0:c0 + W_DIFF], preferred_element_type=F32)

    def head_norm_t(zt, gcol):
        heads = []
        for r in range(0, W_DIFF, HEAD_DIM):
            blk = zt[r:r + HEAD_DIM]
            heads.append(blk * lax.rsqrt(jnp.mean(blk * blk, axis=0, keepdims=True) + EPS) * gcol)
        return heads

    def rope_t(blk):
        x1, x2 = blk[:half], blk[half:]
        return jnp.concatenate([x1 * cost_ref[...] - x2 * sint_ref[...], x1 * sint_ref[...] + x2 * cost_ref[...]],
                               axis=0)

    def head_norm(z, g):
        z2 = z * z
        hi = z2.astype(BF16)
        lo = (z2 - hi.astype(F32)).astype(BF16)
        parts = []
        for c in range(0, W_DIFF, 256):
            parts.append(jnp.dot(hi[:, c:c + 256], gsum, preferred_element_type=F32)
                         + jnp.dot(lo[:, c:c + 256], gsum, preferred_element_type=F32))
        ss = jnp.concatenate(parts, axis=-1)
        return z * lax.rsqrt(ss * (1.0 / HEAD_DIM) + EPS) * g

    cos = jnp.concatenate([cos_ref[...]] * 4, axis=-1)
    sin = jnp.concatenate([sin_ref[...]] * 4, axis=-1)
    lane = lax.broadcasted_iota(jnp.int32, (1, W_DIFF), 1)
    first_half = (lane % HEAD_DIM) < (HEAD_DIM // 2)

    def rope(y):
        partner = jnp.where(first_half, pltpu.roll(y, W_DIFF - HEAD_DIM // 2, 1), pltpu.roll(y, HEAD_DIM // 2, 1))
        return y * cos + partner * sin

    scale = HEAD_DIM ** -0.5 * LOG2E

    def emit_qa(z):
        qa = [rope_t(blk) for blk in head_norm_t(z.T, gqa_ref[...])]
        qt_ref[0] = (jnp.concatenate(qa, axis=0) * scale).astype(BF16)

    def emit_ka(z):
        k_ref[0] = rope(head_norm(z, gka_ref[...])).astype(BF16)

    def emit_va(z):
        vt_ref[0] = z.T.astype(BF16)

    def emit_qn(z):
        qnt_ref[0] = (jnp.concatenate(head_norm_t(z.T, gqn_ref[...]), axis=0) * scale).astype(BF16)

    def emit_kn(z):
        kn_ref[0] = head_norm(z, gkn_ref[...]).astype(BF16)

    def emit_vn(z):
        vnt_ref[0] = z.T.astype(BF16)

    emitters = (emit_qa, emit_ka, emit_va, emit_qn, emit_kn, emit_vn)
    z_next = proj(0)
    for i, emit in enumerate(emitters):
        z = z_next
        if i + 1 < len(emitters):
            z_next = proj((i + 1) * W_DIFF)
        emit(z)


def _proj(x, layer, gmix, w_in, gqa, gka, gqn, gkn, cos, sin, cost, sint, gsum, tm):
    b, s, _ = x.shape
    tok = lambda i, j: (i, j, 0)
    tr = lambda i, j: (i, 0, j)
    row_major = jax.ShapeDtypeStruct((b, s, W_DIFF), BF16)
    col_major = jax.ShapeDtypeStruct((b, W_DIFF, s), BF16)
    half = HEAD_DIM // 2
    return pl.pallas_call(
        _proj_kernel,
        grid=(b, s // tm),
        in_specs=[
            pl.BlockSpec((1, tm, D_MODEL), tok),
            _const_spec((1, D_MODEL)),
            _const_spec((None, D_MODEL, ATT_COLS), (layer, 0, 0)),
            _const_spec((HEAD_DIM, 1)), _const_spec((1, W_DIFF)), _const_spec((HEAD_DIM, 1)), _const_spec((1, W_NA)),
            pl.BlockSpec((tm, 128), lambda i, j: (j, 0)),
            pl.BlockSpec((tm, 128), lambda i, j: (j, 0)),
            pl.BlockSpec((half, tm), lambda i, j: (0, j)),
            pl.BlockSpec((half, tm), lambda i, j: (0, j)),
            _const_spec((256, 256)),
        ],
        out_specs=[
            pl.BlockSpec((1, W_DIFF, tm), tr),
            pl.BlockSpec((1, tm, W_DIFF), tok),
            pl.BlockSpec((1, W_DIFF, tm), tr),
            pl.BlockSpec((1, W_NA, tm), tr),
            pl.BlockSpec((1, tm, W_NA), tok),
            pl.BlockSpec((1, W_NA, tm), tr),
        ],
        out_shape=[col_major, row_major, col_major, col_major, row_major, col_major],
        compiler_params=_params(("parallel", "parallel")),
        name="proj",
    )(x, gmix, w_in, gqa, gka, gqn, gkn, cos, sin, cost, sint, gsum)


N_MAPS = 2 * N_HEADS_DIFF
LOG2E = math.log2(math.e)
MAX_STATIC_SHIFT = 60.0
DIFF_KEY_CHUNK = 2048


def _masked_q_group(qt_ref, j, row, lanes=slice(None)):
    grp = qt_ref[0, 128 * (j // 2):128 * (j // 2) + 128, lanes]
    keep = (row >= HEAD_DIM) if (j % 2) else (row < HEAD_DIM)
    return jnp.where(keep, grp, jnp.zeros_like(grp))


def _diff_finish(acc_sc, l_of, lq1_ref, lk1_ref, lq2_ref, lk2_ref, gsub_ref, o_ref, lambda_init):
    lam = (jnp.exp(jnp.sum(lq1_ref[...] * lk1_ref[...])) - jnp.exp(jnp.sum(lq2_ref[...] * lk2_ref[...]))
           + lambda_init)
    for h in range(N_HEADS_DIFF):
        o0 = acc_sc[h] * (1.0 / l_of(h))
        o1 = acc_sc[N_HEADS_DIFF + h] * (lam / l_of(N_HEADS_DIFF + h))
        o = o0 - o1
        o = o * (lax.rsqrt(jnp.mean(o * o, axis=0, keepdims=True) + EPS) * (1.0 - lambda_init)) * gsub_ref[...]
        o_ref[0, :, 128 * h:128 * h + 128] = o.T.astype(o_ref.dtype)


def _diff_online_kernel(shift_ref, qt_ref, k_ref, vt_ref, lq1_ref, lk1_ref, lq2_ref, lk2_ref, gsub_ref, o_ref,
                        qz_sc, m_sc, l_sc, acc_sc, *, lambda_init):
    del shift_ref
    kv = pl.program_id(2)
    tq = qt_ref.shape[2]

    @pl.when(kv == 0)
    def _init():
        m_sc[...] = jnp.full(m_sc.shape, NEG, F32)
        l_sc[...] = jnp.zeros(l_sc.shape, F32)
        acc_sc[...] = jnp.zeros(acc_sc.shape, F32)
        row = lax.broadcasted_iota(jnp.int32, (128, tq), 0)
        for j in range(N_MAPS):
            qz_sc[j] = _masked_q_group(qt_ref, j, row)

    for j in range(N_MAPS):
        h = j % N_HEADS_DIFF
        kp = k_ref[0, :, 128 * (j // 2):128 * (j // 2) + 128]
        s = jnp.dot(kp, qz_sc[j], preferred_element_type=F32)
        m_prev = m_sc[j]
        m_cur = jnp.maximum(m_prev, jnp.max(s, axis=0, keepdims=True))
        alpha = jnp.exp2(m_prev - m_cur)
        p = jnp.exp2(s - m_cur)
        l_sc[j] = alpha * l_sc[j] + jnp.sum(p, axis=0, keepdims=True)
        vt = vt_ref[0, 128 * h:128 * h + 128, :]
        acc_sc[j] = alpha * acc_sc[j] + jnp.dot(vt, p.astype(BF16), preferred_element_type=F32)
        m_sc[j] = m_cur

    @pl.when(kv == pl.num_programs(2) - 1)
    def _finish():
        _diff_finish(acc_sc, lambda j: l_sc[j], lq1_ref, lk1_ref, lq2_ref, lk2_ref, gsub_ref, o_ref, lambda_init)


def _diff_static_kernel(shift_ref, qt_ref, k_ref, vt_ref, lq1_ref, lk1_ref, lq2_ref, lk2_ref, gsub_ref, o_ref,
                        qz_sc, l_sc, acc_sc, *, lambda_init):
    kv = pl.program_id(2)
    tq = qt_ref.shape[2]
    tk = k_ref.shape[1]

    @pl.when(kv == 0)
    def _init():
        l_sc[...] = jnp.zeros(l_sc.shape, F32)
        acc_sc[...] = jnp.zeros(acc_sc.shape, F32)
        row = lax.broadcasted_iota(jnp.int32, (128, tq), 0)
        shift_rows = jnp.where(row == 0, -shift_ref[0], 0.0).astype(BF16)
        for j in range(N_MAPS):
            qz_sc[j, :128, :] = _masked_q_group(qt_ref, j, row)
            qz_sc[j, 128:, :] = shift_rows

    tc = min(DIFF_KEY_CHUNK, tk)
    ones = jnp.ones((tc, 128), BF16)
    units = [(j, c) for j in range(N_MAPS) for c in range(0, tk, tc)]

    def scores(j, c):
        kaug = jnp.concatenate([k_ref[0, c:c + tc, 128 * (j // 2):128 * (j // 2) + 128], ones], axis=1)
        return jnp.dot(kaug, qz_sc[j], preferred_element_type=F32)

    s_next = scores(*units[0])
    for u, (j, c) in enumerate(units):
        h = j % N_HEADS_DIFF
        s = s_next
        if u + 1 < len(units):
            s_next = scores(*units[u + 1])
        p = jnp.exp2(s)
        l_sc[j] += jnp.sum(p.reshape(tc // 8, 8, tq), axis=0)
        vt = vt_ref[0, 128 * h:128 * h + 128, c:c + tc]
        acc_sc[j] += jnp.dot(vt, p.astype(BF16), preferred_element_type=F32)

    @pl.when(kv == pl.num_programs(2) - 1)
    def _finish():
        _diff_finish(acc_sc, lambda j: jnp.sum(l_sc[j], axis=0, keepdims=True),
                     lq1_ref, lk1_ref, lq2_ref, lk2_ref, gsub_ref, o_ref, lambda_init)


def _diff_call(body, scratch, operands, lambda_init, tq, tk):
    qt = operands[1]
    b, _, s = qt.shape
    return pl.pallas_call(
        functools.partial(body, lambda_init=lambda_init),
        grid=(b, s // tq, s // tk),
        in_specs=[
            pl.BlockSpec(memory_space=pltpu.SMEM),
            pl.BlockSpec((1, W_DIFF, tq), lambda i, j, t: (i, 0, j)),
            pl.BlockSpec((1, tk, W_DIFF), lambda i, j, t: (i, t, 0)),
            pl.BlockSpec((1, W_DIFF, tk), lambda i, j, t: (i, 0, t)),
            _const_spec((1, HEAD_DIM)), _const_spec((1, HEAD_DIM)), _const_spec((1, HEAD_DIM)),
            _const_spec((1, HEAD_DIM)),
            _const_spec((2 * HEAD_DIM, 1)),
        ],
        out_specs=pl.BlockSpec((1, tq, W_DIFF), lambda i, j, t: (i, j, 0)),
        out_shape=jax.ShapeDtypeStruct((b, s, W_DIFF), BF16),
        scratch_shapes=scratch,
        compiler_params=_params(("parallel", "parallel", "arbitrary")),
        name=body.__name__.strip("_"),
    )(*operands)


def _diff_attention(shift, qt, k, vt, lq1, lk1, lq2, lk2, gsub, lambda_init, tq, tk):
    operands = (shift, qt, k, vt, lq1, lk1, lq2, lk2, gsub)
    static = functools.partial(
        _diff_call, _diff_static_kernel,
        [pltpu.VMEM((N_MAPS, 256, tq), BF16), pltpu.VMEM((N_MAPS, 8, tq), F32), pltpu.VMEM((N_MAPS, 128, tq), F32)],
        lambda_init=lambda_init, tq=tq, tk=tk)
    online = functools.partial(
        _diff_call, _diff_online_kernel,
        [pltpu.VMEM((N_MAPS, 128, tq), BF16), pltpu.VMEM((N_MAPS, 1, tq), F32), pltpu.VMEM((N_MAPS, 1, tq), F32),
         pltpu.VMEM((N_MAPS, 128, tq), F32)],
        lambda_init=lambda_init, tq=tq, tk=tk)
    return lax.cond(shift[0] <= MAX_STATIC_SHIFT, static, online, operands)


def _na_selectors(rows):
    kh = min(WIN_H, rows)
    assert kh == WIN_H and rows >= NA_KROWS + NA_QROWS
    rsel = np.zeros((3, NA_QROWS, NA_KROWS, 2 * WIN_H - 1), np.float32)
    for v in range(3):
        r0 = (0, NA_QROWS, rows - NA_QROWS)[v]
        ws = min(max(r0 - 4, 0), rows - NA_KROWS)
        for t in range(NA_QROWS):
            r = r0 + t
            rs = min(max(r - kh // 2, 0), rows - kh)
            for i in range(NA_KROWS):
                kr = ws + i
                if rs <= kr < rs + kh:
                    rsel[v, t, i, kr - r + WIN_H - 1] = 1.0
    csel = np.zeros((GRID_W, GRID_W, 2 * WIN_W - 1), np.float32)
    for c in range(GRID_W):
        cs = min(max(c - WIN_W // 2, 0), GRID_W - WIN_W)
        for j in range(cs, cs + WIN_W):
            csel[c, j, j - c + WIN_W - 1] = 1.0
    return rsel, csel


def _na_bias_blocks(na_bias, rows, shifts):
    rsel, csel = _na_selectors(rows)
    depth = na_bias.shape[0]
    n_rel = 2 * WIN_H
    slab = jnp.einsum("lhde,cje->lhdjc", na_bias.astype(F32), csel, precision=lax.Precision.HIGHEST)
    slab = slab * LOG2E - shifts.reshape(-1, 1, 1, 1, 1)
    slab = jnp.where((csel.sum(-1) > 0.5).T, slab, NEG)
    slab = jnp.concatenate([slab, jnp.full_like(slab[:, :, :1], NEG)], axis=2)
    rel_row = np.where(rsel.sum(-1) > 0.5, rsel.argmax(-1), n_rel - 1).astype(np.int32)

    def assemble(rel_ref, slab_ref, o_ref):
        v, i = pl.program_id(1), pl.program_id(2)
        for t in range(NA_QROWS):
            d = rel_ref[(v * NA_QROWS + t) * NA_KROWS + i]
            o_ref[0, 0, :, :, GRID_W * t:GRID_W * (t + 1)] = slab_ref[0, :, d]

    return pl.pallas_call(
        assemble,
        grid_spec=pltpu.PrefetchScalarGridSpec(
            num_scalar_prefetch=1,
            grid=(depth, 3, NA_KROWS),
            in_specs=[pl.BlockSpec((1, N_HEADS_NA, n_rel, GRID_W, GRID_W), lambda l, v, i, rel: (l, 0, 0, 0, 0))],
            out_specs=pl.BlockSpec((1, 1, N_HEADS_NA, GRID_W, NA_QROWS * GRID_W), lambda l, v, i, rel: (l, v, 0, i, 0)),
        ),
        out_shape=jax.ShapeDtypeStruct((depth, 3, N_HEADS_NA, NA_KROWS * GRID_W, NA_QROWS * GRID_W), F32),
        compiler_params=_params(("parallel", "parallel", "parallel")),
        name="natt_bias_blocks",
    )(jnp.asarray(rel_row.reshape(-1)), slab)


def _na_kernel(qt_ref, *refs, subtract_max):
    n = NA_SUBBLOCKS
    k_refs, vt_refs, bias_refs, o_ref = refs[:n], refs[n:2 * n], refs[2 * n:3 * n], refs[3 * n]
    tq = NA_QROWS * GRID_W
    row = lax.broadcasted_iota(jnp.int32, (128, tq), 0)
    units = [(u, h) for u in range(n) for h in range(N_HEADS_NA)]

    def scores(u, h):
        q = _masked_q_group(qt_ref, h, row, slice(u * tq, (u + 1) * tq))
        return jnp.dot(k_refs[u][0, :, 128 * (h // 2):128 * (h // 2) + 128], q,
                       preferred_element_type=F32) + bias_refs[u][0, h]

    outs = []
    s_next = scores(*units[0])
    for i, (u, h) in enumerate(units):
        s = s_next
        if i + 1 < len(units):
            s_next = scores(*units[i + 1])
        if subtract_max:
            s = s - jnp.max(s, axis=0, keepdims=True)
        p = jnp.exp2(s)
        l = jnp.sum(p, axis=0, keepdims=True)
        vt = vt_refs[u][0, HEAD_DIM * h:HEAD_DIM * (h + 1), :]
        outs.append(jnp.dot(vt, p.astype(BF16), preferred_element_type=F32) * (1.0 / l))
        if h == N_HEADS_NA - 1:
            o_ref[0, u * tq:(u + 1) * tq, :] = jnp.concatenate(outs, axis=0).T.astype(o_ref.dtype)
            outs = []


def _na_call(subtract_max, layer, operands):
    qnt, kn, vnt, bias_blocks = operands
    b, s, _ = kn.shape
    rows = s // GRID_W
    nblk = rows // NA_QROWS
    n = NA_SUBBLOCKS
    tq = NA_QROWS * GRID_W
    win = NA_KROWS * GRID_W
    first = lambda g: jnp.clip(g - 1, 0, nblk - 3) * tq
    variant = lambda g: jnp.where(g == 0, 0, jnp.where(g == nblk - 1, 2, 1))
    kblk = lambda u: pl.BlockSpec((pl.Element(1), pl.Element(win), pl.Element(W_NA)),
                                  lambda i, j: (i, first(n * j + u), 0))
    vblk = lambda u: pl.BlockSpec((pl.Element(1), pl.Element(W_NA), pl.Element(win)),
                                  lambda i, j: (i, 0, first(n * j + u)))
    bblk = lambda u: pl.BlockSpec((None, 1, N_HEADS_NA, win, tq),
                                  lambda i, j: (layer, variant(n * j + u), 0, 0, 0), pipeline_mode=pl.Buffered(1))
    assert nblk % n == 0
    subs = range(n)
    return pl.pallas_call(
        functools.partial(_na_kernel, subtract_max=subtract_max),
        grid=(b, nblk // n),
        in_specs=[pl.BlockSpec((1, W_NA, n * tq), lambda i, j: (i, 0, j))]
        + [kblk(u) for u in subs] + [vblk(u) for u in subs] + [bblk(u) for u in subs],
        out_specs=pl.BlockSpec((1, n * tq, W_NA), lambda i, j: (i, j, 0)),
        out_shape=jax.ShapeDtypeStruct((b, s, W_NA), BF16),
        compiler_params=_params(("parallel", "arbitrary")),
        name="natt_online" if subtract_max else "natt_static",
    )(qnt, *([kn] * n), *([vnt] * n), *([bias_blocks] * n))


def _na_attention(shift_is_static, layer, qnt, kn, vnt, bias_blocks):
    return lax.cond(shift_is_static, functools.partial(_na_call, False, layer),
                    functools.partial(_na_call, True, layer), (qnt, kn, vnt, bias_blocks))


def _merge_kernel(x_ref, oa_ref, on_ref, gmix_ref, wga_ref, wgb_ref, wpa_ref, wpb_ref, wo_ref, y_ref):
    x = x_ref[0]
    h = _rms(x, gmix_ref[...]).astype(BF16)
    ga = jnp.dot(h, wga_ref[...], preferred_element_type=F32)
    gb = jnp.dot(h, wgb_ref[...], preferred_element_type=F32)
    pa = jnp.dot(oa_ref[0], wpa_ref[...], preferred_element_type=F32)
    pb = jnp.dot(on_ref[0], wpb_ref[...], preferred_element_type=F32)
    mixed = jax.nn.sigmoid(ga) * pa + jax.nn.sigmoid(gb) * pb
    y_ref[0] = x + jnp.dot(mixed.astype(BF16), wo_ref[...], preferred_element_type=F32)


def _merge(x, oa, on, layer, gmix, w_in, w_pa, w_pb, w_o, tm):
    b, s, _ = x.shape
    tok = lambda i, j: (i, j, 0)
    gate_block = ATT_COLS // D_MODEL
    return pl.pallas_call(
        _merge_kernel,
        grid=(b, s // tm),
        in_specs=[
            pl.BlockSpec((1, tm, D_MODEL), tok),
            pl.BlockSpec((1, tm, W_DIFF), tok),
            pl.BlockSpec((1, tm, W_NA), tok),
            _const_spec((1, D_MODEL)),
            _const_spec((None, D_MODEL, D_MODEL), (layer, 0, gate_block)),
            _const_spec((None, D_MODEL, D_MODEL), (layer, 0, gate_block + 1)),
            _const_spec((None, W_DIFF, D_MODEL), (layer, 0, 0)),
            _const_spec((None, W_NA, D_MODEL), (layer, 0, 0)),
            _const_spec((None, D_MODEL, D_MODEL), (layer, 0, 0)),
        ],
        out_specs=pl.BlockSpec((1, tm, D_MODEL), tok),
        out_shape=jax.ShapeDtypeStruct(x.shape, F32),
        compiler_params=_params(("parallel", "parallel")),
        name="merge",
    )(x, oa, on, gmix, w_in, w_in, w_pa, w_pb, w_o)


def _ffn_kernel(x_ref, xp_ref, xn_ref, g_ref, wup_ref, cw_ref, cb_ref, wdn_ref, y_ref):
    j = pl.program_id(1)
    tm = x_ref.shape[1]
    x = x_ref[0]
    xe = jnp.concatenate([xp_ref[0], x, xn_ref[0]], axis=0)
    row = lax.broadcasted_iota(jnp.int32, (tm + 2 * HALO, 1), 0)
    inside = jnp.logical_and(jnp.logical_or(row >= HALO, j > 0),
                             jnp.logical_or(row < tm + HALO, j < pl.num_programs(1) - 1))
    he = jnp.where(inside, _rms(xe, g_ref[...]), 0.0).astype(BF16)

    def up(c0, n):
        return tuple(jnp.dot(he, wup_ref[:, c:c + n], preferred_element_type=F32)
                     for c in (c0, D_FF + c0))

    def conv(u, c0, n):
        w = cw_ref[:, c0:c0 + n]
        prev = pltpu.roll(u, 1, 0)[HALO:HALO + tm]
        nxt = pltpu.roll(u, tm + 2 * HALO - 1, 0)[HALO:HALO + tm]
        return prev * w[0:1] + u[HALO:HALO + tm] * w[1:2] + nxt * w[2:3] + cb_ref[:, c0:c0 + n]

    def gelu(v):
        c = math.sqrt(2.0 / math.pi)
        half_v = 0.5 * v
        return half_v + half_v * jnp.tanh(v * (c + (c * 0.044715) * (v * v)))

    starts = [sum(FFN_CHUNKS[:i]) for i in range(len(FFN_CHUNKS))]
    acc = x
    u_next = up(starts[0], FFN_CHUNKS[0])
    for i, (c0, n) in enumerate(zip(starts, FFN_CHUNKS)):
        ug, uv = u_next
        if i + 1 < len(FFN_CHUNKS):
            u_next = up(starts[i + 1], FFN_CHUNKS[i + 1])
        act = (gelu(conv(ug, c0, n)) * conv(uv, D_FF + c0, n)).astype(BF16)
        acc = acc + jnp.dot(act, wdn_ref[c0:c0 + n, :], preferred_element_type=F32)
    y_ref[0] = acc


def _ffn(x, layer, g, w_up, conv_w, conv_b, w_down, tm):
    b, s, _ = x.shape
    nb = tm // HALO
    last = s // HALO - 1
    return pl.pallas_call(
        _ffn_kernel,
        grid=(b, s // tm),
        in_specs=[
            pl.BlockSpec((1, tm, D_MODEL), lambda i, j: (i, j, 0)),
            pl.BlockSpec((1, HALO, D_MODEL), lambda i, j: (i, jnp.maximum(j * nb - 1, 0), 0)),
            pl.BlockSpec((1, HALO, D_MODEL), lambda i, j: (i, jnp.minimum((j + 1) * nb, last), 0)),
            _const_spec((1, D_MODEL)),
            _const_spec((None, D_MODEL, 2 * D_FF), (layer, 0, 0)),
            _const_spec((None, 3, 2 * D_FF), (layer, 0, 0)),
            _const_spec((None, 1, 2 * D_FF), (layer, 0, 0)),
            _const_spec((None, D_FF, D_MODEL), (layer, 0, 0)),
        ],
        out_specs=pl.BlockSpec((1, tm, D_MODEL), lambda i, j: (i, j, 0)),
        out_shape=jax.ShapeDtypeStruct(x.shape, F32),
        compiler_params=_params(("parallel", "parallel")),
        name="ffn",
    )(x, x, x, g, w_up, conv_w, conv_b, w_down)


def _rope_tables(seq):
    half = HEAD_DIM // 2
    inv = jnp.power(ROPE_THETA, -jnp.arange(half, dtype=F32) * 2.0 / HEAD_DIM)
    ang = jnp.arange(seq, dtype=F32)[:, None] * inv[None, :]
    cos, sin = jnp.cos(ang), jnp.sin(ang)
    return jnp.tile(cos, (1, 4)), jnp.tile(jnp.concatenate([-sin, sin], axis=-1), (1, 2)), cos.T, sin.T


def _group_sum_matrix():
    idx = np.arange(256) // HEAD_DIM
    return jnp.asarray(idx[:, None] == idx[None, :], dtype=BF16)


def _tiles(seq):
    tm = min(1024, seq)
    tm_ffn = min(512, seq)
    tq = min(1024, seq)
    tk = min(2048, seq)
    return tm, tm_ffn, tq, tk


def _trunk(x, weights, na_bias, na_shift):
    stacked, per_layer = weights
    seq = x.shape[1]
    tm, tm_ffn, tq, tk = _tiles(seq)
    cos, sin, cost, sint = _rope_tables(seq)
    gsum = _group_sum_matrix()
    na_static = na_shift <= MAX_STATIC_SHIFT
    bias_blocks = _na_bias_blocks(na_bias, seq // GRID_W, jnp.where(na_static, na_shift, 0.0))
    for l, w in enumerate(per_layer):
        lambda_init = 0.8 - 0.6 * math.exp(-0.3 * l)
        qt, k, vt, qnt, kn, vnt = _proj(x, l, w["g_mix"], stacked["w_in"], w["gq_a"], w["gk_a"], w["gq_n"],
                                        w["gk_n"], cos, sin, cost, sint, gsum, tm)
        oa = _diff_attention(w["shift"], qt, k, vt, w["lam_q1"], w["lam_k1"], w["lam_q2"], w["lam_k2"], w["g_sub"],
                             lambda_init, tq, tk)
        on = _na_attention(na_static[l], l, qnt, kn, vnt, bias_blocks)
        x = _merge(x, oa, on, l, w["g_mix"], stacked["w_in"], stacked["w_pa"], stacked["w_pb"], stacked["w_o"], tm)
        x = _ffn(x, l, w["g_ffn"], stacked["w_up"], stacked["conv_w"], stacked["conv_b"], stacked["w_down"],
                 tm_ffn)
    return x


def _score_bound(gq, gk):
    bound = (HEAD_DIM ** 0.5) * LOG2E * jnp.max(jnp.abs(gq)) * jnp.max(jnp.abs(gk))
    return jnp.ceil(1.02 * bound.astype(F32)).reshape(1)


def _na_score_bound(gq, gk, na_bias):
    amax = lambda a: jnp.max(jnp.abs(a.astype(F32)).reshape(a.shape[0], -1), axis=1)
    bound = LOG2E * ((HEAD_DIM ** 0.5) * amax(gq) * amax(gk) + amax(na_bias))
    return jnp.ceil(1.02 * bound)


def _layer_weights(g_mix, w_in, gq_a, gk_a, lam_q1, lam_k1, lam_q2, lam_k2, g_sub, gq_n, gk_n,
                   w_pa, w_pb, w_o, g_ffn, w_up, conv_w, conv_b, w_down):
    depth = w_in.shape[0]
    stacked = dict(w_in=w_in.astype(BF16), w_pa=w_pa.astype(BF16), w_pb=w_pb.astype(BF16), w_o=w_o.astype(BF16),
                   w_up=w_up.astype(BF16), w_down=w_down.astype(BF16), conv_w=conv_w.astype(F32),
                   conv_b=conv_b.astype(F32)[:, None, :])
    head_tile = lambda g: jnp.tile(g.astype(F32), W_DIFF // HEAD_DIM)[None, :]
    row = lambda g: g.astype(F32)[None, :]
    col = lambda g: g.astype(F32)[:, None]
    per_layer = []
    for l in range(depth):
        per_layer.append(dict(
            g_mix=row(g_mix[l]), gq_a=col(gq_a[l]), gk_a=head_tile(gk_a[l]), gq_n=col(gq_n[l]),
            gk_n=head_tile(gk_n[l]),
            lam_q1=row(lam_q1[l]), lam_k1=row(lam_k1[l]), lam_q2=row(lam_q2[l]), lam_k2=row(lam_k2[l]),
            g_sub=col(g_sub[l]), shift=_score_bound(gq_a[l], gk_a[l]), g_ffn=row(g_ffn[l])))
    return stacked, per_layer


def kernel(x_prompt, x_sample, g_mix, w_in, gq_a, gk_a, lam_q1, lam_k1, lam_q2, lam_k2, g_sub, gq_n, gk_n,
           na_bias, w_pa, w_pb, w_o, g_ffn, w_up, conv_w, conv_b, w_down):
    weights = _layer_weights(g_mix, w_in, gq_a, gk_a, lam_q1, lam_k1, lam_q2, lam_k2, g_sub, gq_n, gk_n,
                             w_pa, w_pb, w_o, g_ffn, w_up, conv_w, conv_b, w_down)
    na_shift = _na_score_bound(gq_n, gk_n, na_bias)
    return tuple(_trunk(x, weights, na_bias, na_shift) for x in (x_prompt, x_sample))
```

```python
import functools
import math

import numpy as np
import jax
import jax.numpy as jnp
from jax import lax
from jax.experimental import pallas as pl
from jax.experimental.pallas import tpu as pltpu

D_MODEL = 1024
HEAD_DIM = 64
N_HEADS_DIFF = 4
N_HEADS_NA = 8
W_DIFF = N_HEADS_DIFF * 2 * HEAD_DIM
W_NA = N_HEADS_NA * HEAD_DIM
ATT_COLS = 3 * W_DIFF + 3 * W_NA
GRID_W = 64
WIN_H = 8
WIN_W = 16
D_FF = 2816
ROPE_THETA = 10000.0
EPS = 1e-6
NEG = -1e30

NA_QROWS = 4
NA_KROWS = 12
NA_SUBBLOCKS = 2
FFN_CHUNKS = (2304, 512)
HALO = 8

VMEM_LIMIT = 56 * 1024 * 1024

F32 = jnp.float32
BF16 = jnp.bfloat16


def _const_spec(shape, index=None):
    index = (0,) * len(shape) if index is None else tuple(index)
    return pl.BlockSpec(shape, lambda *_: index, pipeline_mode=pl.Buffered(1))


def _params(sem):
    return pltpu.CompilerParams(dimension_semantics=sem, vmem_limit_bytes=VMEM_LIMIT)


def _rms(x, g):
    return x * lax.rsqrt(jnp.mean(x * x, axis=-1, keepdims=True) + EPS) * g


def _proj_kernel(x_ref, gmix_ref, w_ref, gqa_ref, gka_ref, gqn_ref, gkn_ref, cos_ref, sin_ref, cost_ref, sint_ref,
                 gsum_ref, qt_ref, k_ref, vt_ref, qnt_ref, kn_ref, vnt_ref):
    x = x_ref[0]
    h = _rms(x, gmix_ref[...]).astype(BF16)
    gsum = gsum_ref[...]
    half = HEAD_DIM // 2

    def proj(c0):
        return jnp.dot(h, w_ref[:, c0:c0 + W_DIFF], preferred_element_type=F32)

    def head_norm_t(zt, gcol):
        heads = []
        for r in range(0, W_DIFF, HEAD_DIM):
            blk = zt[r:r + HEAD_DIM]
            heads.append(blk * lax.rsqrt(jnp.mean(blk * blk, axis=0, keepdims=True) + EPS) * gcol)
        return heads

    def rope_t(blk):
        x1, x2 = blk[:half], blk[half:]
        return jnp.concatenate([x1 * cost_ref[...] - x2 * sint_ref[...], x1 * sint_ref[...] + x2 * cost_ref[...]],
                               axis=0)

    def head_norm(z, g):
        z2 = z * z
        hi = z2.astype(BF16)
        lo = (z2 - hi.astype(F32)).astype(BF16)
        parts = []
        for c in range(0, W_DIFF, 256):
            parts.append(jnp.dot(hi[:, c:c + 256], gsum, preferred_element_type=F32)
                         + jnp.dot(lo[:, c:c + 256], gsum, preferred_element_type=F32))
        ss = jnp.concatenate(parts, axis=-1)
        return z * lax.rsqrt(ss * (1.0 / HEAD_DIM) + EPS) * g

    cos = jnp.concatenate([cos_ref[...]] * 4, axis=-1)
    sin = jnp.concatenate([sin_ref[...]] * 4, axis=-1)
    lane = lax.broadcasted_iota(jnp.int32, (1, W_DIFF), 1)
    first_half = (lane % HEAD_DIM) < (HEAD_DIM // 2)

    def rope(y):
        partner = jnp.where(first_half, pltpu.roll(y, W_DIFF - HEAD_DIM // 2, 1), pltpu.roll(y, HEAD_DIM // 2, 1))
        return y * cos + partner * sin

    scale = HEAD_DIM ** -0.5 * LOG2E

    def emit_qa(z):
        qa = [rope_t(blk) for blk in head_norm_t(z.T, gqa_ref[...])]
        qt_ref[0] = (jnp.concatenate(qa, axis=0) * scale).astype(BF16)

    def emit_ka(z):
        k_ref[0] = rope(head_norm(z, gka_ref[...])).astype(BF16)

    def emit_va(z):
        vt_ref[0] = z.T.astype(BF16)

    def emit_qn(z):
        qnt_ref[0] = (jnp.concatenate(head_norm_t(z.T, gqn_ref[...]), axis=0) * scale).astype(BF16)

    def emit_kn(z):
        kn_ref[0] = head_norm(z, gkn_ref[...]).astype(BF16)

    def emit_vn(z):
        vnt_ref[0] = z.T.astype(BF16)

    emitters = (emit_qa, emit_ka, emit_va, emit_qn, emit_kn, emit_vn)
    z_next = proj(0)
    for i, emit in enumerate(emitters):
        z = z_next
        if i + 1 < len(emitters):
            z_next = proj((i + 1) * W_DIFF)
        emit(z)


def _proj(x, layer, gmix, w_in, gqa, gka, gqn, gkn, cos, sin, cost, sint, gsum, tm):
    b, s, _ = x.shape
    tok = lambda i, j: (i, j, 0)
    tr = lambda i, j: (i, 0, j)
    row_major = jax.ShapeDtypeStruct((b, s, W_DIFF), BF16)
    col_major = jax.ShapeDtypeStruct((b, W_DIFF, s), BF16)
    half = HEAD_DIM // 2
    return pl.pallas_call(
        _proj_kernel,
        grid=(b, s // tm),
        in_specs=[
            pl.BlockSpec((1, tm, D_MODEL), tok),
            _const_spec((1, D_MODEL)),
            _const_spec((None, D_MODEL, ATT_COLS), (layer, 0, 0)),
            _const_spec((HEAD_DIM, 1)), _const_spec((1, W_DIFF)), _const_spec((HEAD_DIM, 1)), _const_spec((1, W_NA)),
            pl.BlockSpec((tm, 128), lambda i, j: (j, 0)),
            pl.BlockSpec((tm, 128), lambda i, j: (j, 0)),
            pl.BlockSpec((half, tm), lambda i, j: (0, j)),
            pl.BlockSpec((half, tm), lambda i, j: (0, j)),
            _const_spec((256, 256)),
        ],
        out_specs=[
            pl.BlockSpec((1, W_DIFF, tm), tr),
            pl.BlockSpec((1, tm, W_DIFF), tok),
            pl.BlockSpec((1, W_DIFF, tm), tr),
            pl.BlockSpec((1, W_NA, tm), tr),
            pl.BlockSpec((1, tm, W_NA), tok),
            pl.BlockSpec((1, W_NA, tm), tr),
        ],
        out_shape=[col_major, row_major, col_major, col_major, row_major, col_major],
        compiler_params=_params(("parallel", "parallel")),
        name="proj",
    )(x, gmix, w_in, gqa, gka, gqn, gkn, cos, sin, cost, sint, gsum)


N_MAPS = 2 * N_HEADS_DIFF
LOG2E = math.log2(math.e)
MAX_STATIC_SHIFT = 60.0
DIFF_KEY_CHUNK = 2048


def _masked_q_group(qt_ref, j, row, lanes=slice(None)):
    grp = qt_ref[0, 128 * (j // 2):128 * (j // 2) + 128, lanes]
    keep = (row >= HEAD_DIM) if (j % 2) else (row < HEAD_DIM)
    return jnp.where(keep, grp, jnp.zeros_like(grp))


def _diff_finish(acc_sc, l_of, lq1_ref, lk1_ref, lq2_ref, lk2_ref, gsub_ref, o_ref, lambda_init):
    lam = (jnp.exp(jnp.sum(lq1_ref[...] * lk1_ref[...])) - jnp.exp(jnp.sum(lq2_ref[...] * lk2_ref[...]))
           + lambda_init)
    for h in range(N_HEADS_DIFF):
        o0 = acc_sc[h] * (1.0 / l_of(h))
        o1 = acc_sc[N_HEADS_DIFF + h] * (lam / l_of(N_HEADS_DIFF + h))
        o = o0 - o1
        o = o * (lax.rsqrt(jnp.mean(o * o, axis=0, keepdims=True) + EPS) * (1.0 - lambda_init)) * gsub_ref[...]
        o_ref[0, :, 128 * h:128 * h + 128] = o.T.astype(o_ref.dtype)


def _diff_online_kernel(shift_ref, qt_ref, k_ref, vt_ref, lq1_ref, lk1_ref, lq2_ref, lk2_ref, gsub_ref, o_ref,
                        qz_sc, m_sc, l_sc, acc_sc, *, lambda_init):
    del shift_ref
    kv = pl.program_id(2)
    tq = qt_ref.shape[2]

    @pl.when(kv == 0)
    def _init():
        m_sc[...] = jnp.full(m_sc.shape, NEG, F32)
        l_sc[...] = jnp.zeros(l_sc.shape, F32)
        acc_sc[...] = jnp.zeros(acc_sc.shape, F32)
        row = lax.broadcasted_iota(jnp.int32, (128, tq), 0)
        for j in range(N_MAPS):
            qz_sc[j] = _masked_q_group(qt_ref, j, row)

    for j in range(N_MAPS):
        h = j % N_HEADS_DIFF
        kp = k_ref[0, :, 128 * (j // 2):128 * (j // 2) + 128]
        s = jnp.dot(kp, qz_sc[j], preferred_element_type=F32)
        m_prev = m_sc[j]
        m_cur = jnp.maximum(m_prev, jnp.max(s, axis=0, keepdims=True))
        alpha = jnp.exp2(m_prev - m_cur)
        p = jnp.exp2(s - m_cur)
        l_sc[j] = alpha * l_sc[j] + jnp.sum(p, axis=0, keepdims=True)
        vt = vt_ref[0, 128 * h:128 * h + 128, :]
        acc_sc[j] = alpha * acc_sc[j] + jnp.dot(vt, p.astype(BF16), preferred_element_type=F32)
        m_sc[j] = m_cur

    @pl.when(kv == pl.num_programs(2) - 1)
    def _finish():
        _diff_finish(acc_sc, lambda j: l_sc[j], lq1_ref, lk1_ref, lq2_ref, lk2_ref, gsub_ref, o_ref, lambda_init)


def _diff_static_kernel(shift_ref, qt_ref, k_ref, vt_ref, lq1_ref, lk1_ref, lq2_ref, lk2_ref, gsub_ref, o_ref,
                        qz_sc, l_sc, acc_sc, *, lambda_init):
    kv = pl.program_id(2)
    tq = qt_ref.shape[2]
    tk = k_ref.shape[1]

    @pl.when(kv == 0)
    def _init():
        l_sc[...] = jnp.zeros(l_sc.shape, F32)
        acc_sc[...] = jnp.zeros(acc_sc.shape, F32)
        row = lax.broadcasted_iota(jnp.int32, (128, tq), 0)
        shift_rows = jnp.where(row == 0, -shift_ref[0], 0.0).astype(BF16)
        for j in range(N_MAPS):
            qz_sc[j, :128, :] = _masked_q_group(qt_ref, j, row)
            qz_sc[j, 128:, :] = shift_rows

    tc = min(DIFF_KEY_CHUNK, tk)
    ones = jnp.ones((tc, 128), BF16)
    units = [(j, c) for j in range(N_MAPS) for c in range(0, tk, tc)]

    def scores(j, c):
        kaug = jnp.concatenate([k_ref[0, c:c + tc, 128 * (j // 2):128 * (j // 2) + 128], ones], axis=1)
        return jnp.dot(kaug, qz_sc[j], preferred_element_type=F32)

    s_next = scores(*units[0])
    for u, (j, c) in enumerate(units):
        h = j % N_HEADS_DIFF
        s = s_next
        if u + 1 < len(units):
            s_next = scores(*units[u + 1])
        p = jnp.exp2(s)
        l_sc[j] += jnp.sum(p.reshape(tc // 8, 8, tq), axis=0)
        vt = vt_ref[0, 128 * h:128 * h + 128, c:c + tc]
        acc_sc[j] += jnp.dot(vt, p.astype(BF16), preferred_element_type=F32)

    @pl.when(kv == pl.num_programs(2) - 1)
    def _finish():
        _diff_finish(acc_sc, lambda j: jnp.sum(l_sc[j], axis=0, keepdims=True),
                     lq1_ref, lk1_ref, lq2_ref, lk2_ref, gsub_ref, o_ref, lambda_init)


def _diff_call(body, scratch, operands, lambda_init, tq, tk):
    qt = operands[1]
    b, _, s = qt.shape
    return pl.pallas_call(
        functools.partial(body, lambda_init=lambda_init),
        grid=(b, s // tq, s // tk),
        in_specs=[
            pl.BlockSpec(memory_space=pltpu.SMEM),
            pl.BlockSpec((1, W_DIFF, tq), lambda i, j, t: (i, 0, j)),
            pl.BlockSpec((1, tk, W_DIFF), lambda i, j, t: (i, t, 0)),
            pl.BlockSpec((1, W_DIFF, tk), lambda i, j, t: (i, 0, t)),
            _const_spec((1, HEAD_DIM)), _const_spec((1, HEAD_DIM)), _const_spec((1, HEAD_DIM)),
            _const_spec((1, HEAD_DIM)),
            _const_spec((2 * HEAD_DIM, 1)),
        ],
        out_specs=pl.BlockSpec((1, tq, W_DIFF), lambda i, j, t: (i, j, 0)),
        out_shape=jax.ShapeDtypeStruct((b, s, W_DIFF), BF16),
        scratch_shapes=scratch,
        compiler_params=_params(("parallel", "parallel", "arbitrary")),
        name=body.__name__.strip("_"),
    )(*operands)


def _diff_attention(shift, qt, k, vt, lq1, lk1, lq2, lk2, gsub, lambda_init, tq, tk):
    operands = (shift, qt, k, vt, lq1, lk1, lq2, lk2, gsub)
    static = functools.partial(
        _diff_call, _diff_static_kernel,
        [pltpu.VMEM((N_MAPS, 256, tq), BF16), pltpu.VMEM((N_MAPS, 8, tq), F32), pltpu.VMEM((N_MAPS, 128, tq), F32)],
        lambda_init=lambda_init, tq=tq, tk=tk)
    online = functools.partial(
        _diff_call, _diff_online_kernel,
        [pltpu.VMEM((N_MAPS, 128, tq), BF16), pltpu.VMEM((N_MAPS, 1, tq), F32), pltpu.VMEM((N_MAPS, 1, tq), F32),
         pltpu.VMEM((N_MAPS, 128, tq), F32)],
        lambda_init=lambda_init, tq=tq, tk=tk)
    return lax.cond(shift[0] <= MAX_STATIC_SHIFT, static, online, operands)


def _na_selectors(rows):
    kh = min(WIN_H, rows)
    assert kh == WIN_H and rows >= NA_KROWS + NA_QROWS
    rsel = np.zeros((3, NA_QROWS, NA_KROWS, 2 * WIN_H - 1), np.float32)
    for v in range(3):
        r0 = (0, NA_QROWS, rows - NA_QROWS)[v]
        ws = min(max(r0 - 4, 0), rows - NA_KROWS)
        for t in range(NA_QROWS):
            r = r0 + t
            rs = min(max(r - kh // 2, 0), rows - kh)
            for i in range(NA_KROWS):
                kr = ws + i
                if rs <= kr < rs + kh:
                    rsel[v, t, i, kr - r + WIN_H - 1] = 1.0
    csel = np.zeros((GRID_W, GRID_W, 2 * WIN_W - 1), np.float32)
    for c in range(GRID_W):
        cs = min(max(c - WIN_W // 2, 0), GRID_W - WIN_W)
        for j in range(cs, cs + WIN_W):
            csel[c, j, j - c + WIN_W - 1] = 1.0
    return rsel, csel


def _na_bias_blocks(na_bias, rows, shifts):
    rsel, csel = _na_selectors(rows)
    depth = na_bias.shape[0]
    n_rel = 2 * WIN_H
    slab = jnp.einsum("lhde,cje->lhdjc", na_bias.astype(F32), csel, precision=lax.Precision.HIGHEST)
    slab = slab * LOG2E - shifts.reshape(-1, 1, 1, 1, 1)
    slab = jnp.where((csel.sum(-1) > 0.5).T, slab, NEG)
    slab = jnp.concatenate([slab, jnp.full_like(slab[:, :, :1], NEG)], axis=2)
    rel_row = np.where(rsel.sum(-1) > 0.5, rsel.argmax(-1), n_rel - 1).astype(np.int32)

    def assemble(rel_ref, slab_ref, o_ref):
        v, i = pl.program_id(1), pl.program_id(2)
        for t in range(NA_QROWS):
            d = rel_ref[(v * NA_QROWS + t) * NA_KROWS + i]
            o_ref[0, 0, :, :, GRID_W * t:GRID_W * (t + 1)] = slab_ref[0, :, d]

    return pl.pallas_call(
        assemble,
        grid_spec=pltpu.PrefetchScalarGridSpec(
            num_scalar_prefetch=1,
            grid=(depth, 3, NA_KROWS),
            in_specs=[pl.BlockSpec((1, N_HEADS_NA, n_rel, GRID_W, GRID_W), lambda l, v, i, rel: (l, 0, 0, 0, 0))],
            out_specs=pl.BlockSpec((1, 1, N_HEADS_NA, GRID_W, NA_QROWS * GRID_W), lambda l, v, i, rel: (l, v, 0, i, 0)),
        ),
        out_shape=jax.ShapeDtypeStruct((depth, 3, N_HEADS_NA, NA_KROWS * GRID_W, NA_QROWS * GRID_W), F32),
        compiler_params=_params(("parallel", "parallel", "parallel")),
        name="natt_bias_blocks",
    )(jnp.asarray(rel_row.reshape(-1)), slab)


def _na_kernel(qt_ref, *refs, subtract_max):
    n = NA_SUBBLOCKS
    k_refs, vt_refs, bias_refs, o_ref = refs[:n], refs[n:2 * n], refs[2 * n:3 * n], refs[3 * n]
    tq = NA_QROWS * GRID_W
    row = lax.broadcasted_iota(jnp.int32, (128, tq), 0)
    units = [(u, h) for u in range(n) for h in range(N_HEADS_NA)]

    def scores(u, h):
        q = _masked_q_group(qt_ref, h, row, slice(u * tq, (u + 1) * tq))
        return jnp.dot(k_refs[u][0, :, 128 * (h // 2):128 * (h // 2) + 128], q,
                       preferred_element_type=F32) + bias_refs[u][0, h]

    outs = []
    s_next = scores(*units[0])
    for i, (u, h) in enumerate(units):
        s = s_next
        if i + 1 < len(units):
            s_next = scores(*units[i + 1])
        if subtract_max:
            s = s - jnp.max(s, axis=0, keepdims=True)
        p = jnp.exp2(s)
        l = jnp.sum(p, axis=0, keepdims=True)
        vt = vt_refs[u][0, HEAD_DIM * h:HEAD_DIM * (h + 1), :]
        outs.append(jnp.dot(vt, p.astype(BF16), preferred_element_type=F32) * (1.0 / l))
        if h == N_HEADS_NA - 1:
            o_ref[0, u * tq:(u + 1) * tq, :] = jnp.concatenate(outs, axis=0).T.astype(o_ref.dtype)
            outs = []


def _na_call(subtract_max, layer, operands):
    qnt, kn, vnt, bias_blocks = operands
    b, s, _ = kn.shape
    rows = s // GRID_W
    nblk = rows // NA_QROWS
    n = NA_SUBBLOCKS
    tq = NA_QROWS * GRID_W
    win = NA_KROWS * GRID_W
    first = lambda g: jnp.clip(g - 1, 0, nblk - 3) * tq
    variant = lambda g: jnp.where(g == 0, 0, jnp.where(g == nblk - 1, 2, 1))
    kblk = lambda u: pl.BlockSpec((pl.Element(1), pl.Element(win), pl.Element(W_NA)),
                                  lambda i, j: (i, first(n * j + u), 0))
    vblk = lambda u: pl.BlockSpec((pl.Element(1), pl.Element(W_NA), pl.Element(win)),
                                  lambda i, j: (i, 0, first(n * j + u)))
    bblk = lambda u: pl.BlockSpec((None, 1, N_HEADS_NA, win, tq),
                                  lambda i, j: (layer, variant(n * j + u), 0, 0, 0))
    assert nblk % n == 0
    subs = range(n)
    return pl.pallas_call(
        functools.partial(_na_kernel, subtract_max=subtract_max),
        grid=(b, nblk // n),
        in_specs=[pl.BlockSpec((1, W_NA, n * tq), lambda i, j: (i, 0, j))]
        + [kblk(u) for u in subs] + [vblk(u) for u in subs] + [bblk(u) for u in subs],
        out_specs=pl.BlockSpec((1, n * tq, W_NA), lambda i, j: (i, j, 0)),
        out_shape=jax.ShapeDtypeStruct((b, s, W_NA), BF16),
        compiler_params=_params(("parallel", "arbitrary")),
        name="natt_online" if subtract_max else "natt_static",
    )(qnt, *([kn] * n), *([vnt] * n), *([bias_blocks] * n))


def _na_attention(shift_is_static, layer, qnt, kn, vnt, bias_blocks):
    return lax.cond(shift_is_static, functools.partial(_na_call, False, layer),
                    functools.partial(_na_call, True, layer), (qnt, kn, vnt, bias_blocks))


def _merge_kernel(x_ref, oa_ref, on_ref, gmix_ref, wga_ref, wgb_ref, wpa_ref, wpb_ref, wo_ref, y_ref):
    x = x_ref[0]
    h = _rms(x, gmix_ref[...]).astype(BF16)
    ga = jnp.dot(h, wga_ref[...], preferred_element_type=F32)
    gb = jnp.dot(h, wgb_ref[...], preferred_element_type=F32)
    pa = jnp.dot(oa_ref[0], wpa_ref[...], preferred_element_type=F32)
    pb = jnp.dot(on_ref[0], wpb_ref[...], preferred_element_type=F32)
    mixed = jax.nn.sigmoid(ga) * pa + jax.nn.sigmoid(gb) * pb
    y_ref[0] = x + jnp.dot(mixed.astype(BF16), wo_ref[...], preferred_element_type=F32)


def _merge(x, oa, on, layer, gmix, w_in, w_pa, w_pb, w_o, tm):
    b, s, _ = x.shape
    tok = lambda i, j: (i, j, 0)
    gate_block = ATT_COLS // D_MODEL
    return pl.pallas_call(
        _merge_kernel,
        grid=(b, s // tm),
        in_specs=[
            pl.BlockSpec((1, tm, D_MODEL), tok),
            pl.BlockSpec((1, tm, W_DIFF), tok),
            pl.BlockSpec((1, tm, W_NA), tok),
            _const_spec((1, D_MODEL)),
            _const_spec((None, D_MODEL, D_MODEL), (layer, 0, gate_block)),
            _const_spec((None, D_MODEL, D_MODEL), (layer, 0, gate_block + 1)),
            _const_spec((None, W_DIFF, D_MODEL), (layer, 0, 0)),
            _const_spec((None, W_NA, D_MODEL), (layer, 0, 0)),
            _const_spec((None, D_MODEL, D_MODEL), (layer, 0, 0)),
        ],
        out_specs=pl.BlockSpec((1, tm, D_MODEL), tok),
        out_shape=jax.ShapeDtypeStruct(x.shape, F32),
        compiler_params=_params(("parallel", "parallel")),
        name="merge",
    )(x, oa, on, gmix, w_in, w_in, w_pa, w_pb, w_o)


def _ffn_kernel(x_ref, xp_ref, xn_ref, g_ref, wup_ref, cw_ref, cb_ref, wdn_ref, y_ref):
    j = pl.program_id(1)
    tm = x_ref.shape[1]
    x = x_ref[0]
    xe = jnp.concatenate([xp_ref[0], x, xn_ref[0]], axis=0)
    row = lax.broadcasted_iota(jnp.int32, (tm + 2 * HALO, 1), 0)
    inside = jnp.logical_and(jnp.logical_or(row >= HALO, j > 0),
                             jnp.logical_or(row < tm + HALO, j < pl.num_programs(1) - 1))
    he = jnp.where(inside, _rms(xe, g_ref[...]), 0.0).astype(BF16)

    def up(c0, n):
        return tuple(jnp.dot(he, wup_ref[:, c:c + n], preferred_element_type=F32)
                     for c in (c0, D_FF + c0))

    def conv(u, c0, n):
        w = cw_ref[:, c0:c0 + n]
        prev = pltpu.roll(u, 1, 0)[HALO:HALO + tm]
        nxt = pltpu.roll(u, tm + 2 * HALO - 1, 0)[HALO:HALO + tm]
        return prev * w[0:1] + u[HALO:HALO + tm] * w[1:2] + nxt * w[2:3] + cb_ref[:, c0:c0 + n]

    def gelu(v):
        c = math.sqrt(2.0 / math.pi)
        half_v = 0.5 * v
        return half_v + half_v * jnp.tanh(v * (c + (c * 0.044715) * (v * v)))

    starts = [sum(FFN_CHUNKS[:i]) for i in range(len(FFN_CHUNKS))]
    acc = x
    u_next = up(starts[0], FFN_CHUNKS[0])
    for i, (c0, n) in enumerate(zip(starts, FFN_CHUNKS)):
        ug, uv = u_next
        if i + 1 < len(FFN_CHUNKS):
            u_next = up(starts[i + 1], FFN_CHUNKS[i + 1])
        act = (gelu(conv(ug, c0, n)) * conv(uv, D_FF + c0, n)).astype(BF16)
        acc = acc + jnp.dot(act, wdn_ref[c0:c0 + n, :], preferred_element_type=F32)
    y_ref[0] = acc


def _ffn(x, layer, g, w_up, conv_w, conv_b, w_down, tm):
    b, s, _ = x.shape
    nb = tm // HALO
    last = s // HALO - 1
    return pl.pallas_call(
        _ffn_kernel,
        grid=(b, s // tm),
        in_specs=[
            pl.BlockSpec((1, tm, D_MODEL), lambda i, j: (i, j, 0)),
            pl.BlockSpec((1, HALO, D_MODEL), lambda i, j: (i, jnp.maximum(j * nb - 1, 0), 0)),
            pl.BlockSpec((1, HALO, D_MODEL), lambda i, j: (i, jnp.minimum((j + 1) * nb, last), 0)),
            _const_spec((1, D_MODEL)),
            _const_spec((None, D_MODEL, 2 * D_FF), (layer, 0, 0)),
            _const_spec((None, 3, 2 * D_FF), (layer, 0, 0)),
            _const_spec((None, 1, 2 * D_FF), (layer, 0, 0)),
            _const_spec((None, D_FF, D_MODEL), (layer, 0, 0)),
        ],
        out_specs=pl.BlockSpec((1, tm, D_MODEL), lambda i, j: (i, j, 0)),
        out_shape=jax.ShapeDtypeStruct(x.shape, F32),
        compiler_params=_params(("parallel", "parallel")),
        name="ffn",
    )(x, x, x, g, w_up, conv_w, conv_b, w_down)


def _rope_tables(seq):
    half = HEAD_DIM // 2
    inv = jnp.power(ROPE_THETA, -jnp.arange(half, dtype=F32) * 2.0 / HEAD_DIM)
    ang = jnp.arange(seq, dtype=F32)[:, None] * inv[None, :]
    cos, sin = jnp.cos(ang), jnp.sin(ang)
    return jnp.tile(cos, (1, 4)), jnp.tile(jnp.concatenate([-sin, sin], axis=-1), (1, 2)), cos.T, sin.T


def _group_sum_matrix():
    idx = np.arange(256) // HEAD_DIM
    return jnp.asarray(idx[:, None] == idx[None, :], dtype=BF16)


def _tiles(seq):
    tm = min(1024, seq)
    tm_ffn = min(512, seq)
    tq = min(1024, seq)
    tk = min(2048, seq)
    return tm, tm_ffn, tq, tk


def _trunk(x, weights, na_bias, na_shift):
    stacked, per_layer = weights
    seq = x.shape[1]
    tm, tm_ffn, tq, tk = _tiles(seq)
    cos, sin, cost, sint = _rope_tables(seq)
    gsum = _group_sum_matrix()
    na_static = na_shift <= MAX_STATIC_SHIFT
    bias_blocks = _na_bias_blocks(na_bias, seq // GRID_W, jnp.where(na_static, na_shift, 0.0))
    for l, w in enumerate(per_layer):
        lambda_init = 0.8 - 0.6 * math.exp(-0.3 * l)
        qt, k, vt, qnt, kn, vnt = _proj(x, l, w["g_mix"], stacked["w_in"], w["gq_a"], w["gk_a"], w["gq_n"],
                                        w["gk_n"], cos, sin, cost, sint, gsum, tm)
        oa = _diff_attention(w["shift"], qt, k, vt, w["lam_q1"], w["lam_k1"], w["lam_q2"], w["lam_k2"], w["g_sub"],
                             lambda_init, tq, tk)
        on = _na_attention(na_static[l], l, qnt, kn, vnt, bias_blocks)
        x = _merge(x, oa, on, l, w["g_mix"], stacked["w_in"], stacked["w_pa"], stacked["w_pb"], stacked["w_o"], tm)
        x = _ffn(x, l, w["g_ffn"], stacked["w_up"], stacked["conv_w"], stacked["conv_b"], stacked["w_down"],
                 tm_ffn)
    return x


def _score_bound(gq, gk):
    bound = (HEAD_DIM ** 0.5) * LOG2E * jnp.max(jnp.abs(gq)) * jnp.max(jnp.abs(gk))
    return jnp.ceil(1.02 * bound.astype(F32)).reshape(1)


def _na_score_bound(gq, gk, na_bias):
    amax = lambda a: jnp.max(jnp.abs(a.astype(F32)).reshape(a.shape[0], -1), axis=1)
    bound = LOG2E * ((HEAD_DIM ** 0.5) * amax(gq) * amax(gk) + amax(na_bias))
    return jnp.ceil(1.02 * bound)


def _layer_weights(g_mix, w_in, gq_a, gk_a, lam_q1, lam_k1, lam_q2, lam_k2, g_sub, gq_n, gk_n,
                   w_pa, w_pb, w_o, g_ffn, w_up, conv_w, conv_b, w_down):
    depth = w_in.shape[0]
    stacked = dict(w_in=w_in.astype(BF16), w_pa=w_pa.astype(BF16), w_pb=w_pb.astype(BF16), w_o=w_o.astype(BF16),
                   w_up=w_up.astype(BF16), w_down=w_down.astype(BF16), conv_w=conv_w.astype(F32),
                   conv_b=conv_b.astype(F32)[:, None, :])
    head_tile = lambda g: jnp.tile(g.astype(F32), W_DIFF // HEAD_DIM)[None, :]
    row = lambda g: g.astype(F32)[None, :]
    col = lambda g: g.astype(F32)[:, None]
    per_layer = []
    for l in range(depth):
        per_layer.append(dict(
            g_mix=row(g_mix[l]), gq_a=col(gq_a[l]), gk_a=head_tile(gk_a[l]), gq_n=col(gq_n[l]),
            gk_n=head_tile(gk_n[l]),
            lam_q1=row(lam_q1[l]), lam_k1=row(lam_k1[l]), lam_q2=row(lam_q2[l]), lam_k2=row(lam_k2[l]),
            g_sub=col(g_sub[l]), shift=_score_bound(gq_a[l], gk_a[l]), g_ffn=row(g_ffn[l])))
    return stacked, per_layer


def kernel(x_prompt, x_sample, g_mix, w_in, gq_a, gk_a, lam_q1, lam_k1, lam_q2, lam_k2, g_sub, gq_n, gk_n,
           na_bias, w_pa, w_pb, w_o, g_ffn, w_up, conv_w, conv_b, w_down):
    weights = _layer_weights(g_mix, w_in, gq_a, gk_a, lam_q1, lam_k1, lam_q2, lam_k2, g_sub, gq_n, gk_n,
                             w_pa, w_pb, w_o, g_ffn, w_up, conv_w, conv_b, w_down)
    na_shift = _na_score_bound(gq_n, gk_n, na_bias)
    return tuple(_trunk(x, weights, na_bias, na_shift) for x in (x_prompt, x_sample))
```

```python
import functools
import math

import numpy as np
import jax
import jax.numpy as jnp
from jax import lax
from jax.experimental import pallas as pl
from jax.experimental.pallas import tpu as pltpu

D_MODEL = 1024
HEAD_DIM = 64
N_HEADS_DIFF = 4
N_HEADS_NA = 8
W_DIFF = N_HEADS_DIFF * 2 * HEAD_DIM
W_NA = N_HEADS_NA * HEAD_DIM
ATT_COLS = 3 * W_DIFF + 3 * W_NA
GRID_W = 64
WIN_H = 8
WIN_W = 16
D_FF = 2816
ROPE_THETA = 10000.0
EPS = 1e-6
NEG = -1e30

NA_QROWS = 4
NA_KROWS = 12
NA_SUBBLOCKS = 4
FFN_CHUNKS = (2304, 512)
HALO = 8

VMEM_LIMIT = 56 * 1024 * 1024

F32 = jnp.float32
BF16 = jnp.bfloat16


def _const_spec(shape, index=None):
    index = (0,) * len(shape) if index is None else tuple(index)
    return pl.BlockSpec(shape, lambda *_: index, pipeline_mode=pl.Buffered(1))


def _params(sem):
    return pltpu.CompilerParams(dimension_semantics=sem, vmem_limit_bytes=VMEM_LIMIT)


def _rms(x, g):
    return x * lax.rsqrt(jnp.mean(x * x, axis=-1, keepdims=True) + EPS) * g


def _proj_kernel(x_ref, gmix_ref, w_ref, gqa_ref, gka_ref, gqn_ref, gkn_ref, cos_ref, sin_ref, cost_ref, sint_ref,
                 gsum_ref, qt_ref, k_ref, vt_ref, qnt_ref, kn_ref, vnt_ref):
    x = x_ref[0]
    h = _rms(x, gmix_ref[...]).astype(BF16)
    gsum = gsum_ref[...]
    half = HEAD_DIM // 2

    def proj(c0):
        return jnp.dot(h, w_ref[:, c0:c0 + W_DIFF], preferred_element_type=F32)

    def head_norm_t(zt, gcol):
        heads = []
        for r in range(0, W_DIFF, HEAD_DIM):
            blk = zt[r:r + HEAD_DIM]
            heads.append(blk * lax.rsqrt(jnp.mean(blk * blk, axis=0, keepdims=True) + EPS) * gcol)
        return heads

    def rope_t(blk):
        x1, x2 = blk[:half], blk[half:]
        return jnp.concatenate([x1 * cost_ref[...] - x2 * sint_ref[...], x1 * sint_ref[...] + x2 * cost_ref[...]],
                               axis=0)

    def head_norm(z, g):
        z2 = z * z
        hi = z2.astype(BF16)
        lo = (z2 - hi.astype(F32)).astype(BF16)
        parts = []
        for c in range(0, W_DIFF, 256):
            parts.append(jnp.dot(hi[:, c:c + 256], gsum, preferred_element_type=F32)
                         + jnp.dot(lo[:, c:c + 256], gsum, preferred_element_type=F32))
        ss = jnp.concatenate(parts, axis=-1)
        return z * lax.rsqrt(ss * (1.0 / HEAD_DIM) + EPS) * g

    cos = jnp.concatenate([cos_ref[...]] * 4, axis=-1)
    sin = jnp.concatenate([sin_ref[...]] * 4, axis=-1)
    lane = lax.broadcasted_iota(jnp.int32, (1, W_DIFF), 1)
    first_half = (lane % HEAD_DIM) < (HEAD_DIM // 2)

    def rope(y):
        partner = jnp.where(first_half, pltpu.roll(y, W_DIFF - HEAD_DIM // 2, 1), pltpu.roll(y, HEAD_DIM // 2, 1))
        return y * cos + partner * sin

    scale = HEAD_DIM ** -0.5 * LOG2E

    def emit_qa(z):
        qa = [rope_t(blk) for blk in head_norm_t(z.T, gqa_ref[...])]
        qt_ref[0] = (jnp.concatenate(qa, axis=0) * scale).astype(BF16)

    def emit_ka(z):
        k_ref[0] = rope(head_norm(z, gka_ref[...])).astype(BF16)

    def emit_va(z):
        vt_ref[0] = z.T.astype(BF16)

    def emit_qn(z):
        qnt_ref[0] = (jnp.concatenate(head_norm_t(z.T, gqn_ref[...]), axis=0) * scale).astype(BF16)

    def emit_kn(z):
        kn_ref[0] = head_norm(z, gkn_ref[...]).astype(BF16)

    def emit_vn(z):
        vnt_ref[0] = z.T.astype(BF16)

    emitters = (emit_qa, emit_ka, emit_va, emit_qn, emit_kn, emit_vn)
    z_next = proj(0)
    for i, emit in enumerate(emitters):
        z = z_next
        if i + 1 < len(emitters):
            z_next = proj((i + 1) * W_DIFF)
        emit(z)


def _proj(x, layer, gmix, w_in, gqa, gka, gqn, gkn, cos, sin, cost, sint, gsum, tm):
    b, s, _ = x.shape
    tok = lambda i, j: (i, j, 0)
    tr = lambda i, j: (i, 0, j)
    row_major = jax.ShapeDtypeStruct((b, s, W_DIFF), BF16)
    col_major = jax.ShapeDtypeStruct((b, W_DIFF, s), BF16)
    half = HEAD_DIM // 2
    return pl.pallas_call(
        _proj_kernel,
        grid=(b, s // tm),
        in_specs=[
            pl.BlockSpec((1, tm, D_MODEL), tok),
            _const_spec((1, D_MODEL)),
            _const_spec((None, D_MODEL, ATT_COLS), (layer, 0, 0)),
            _const_spec((HEAD_DIM, 1)), _const_spec((1, W_DIFF)), _const_spec((HEAD_DIM, 1)), _const_spec((1, W_NA)),
            pl.BlockSpec((tm, 128), lambda i, j: (j, 0)),
            pl.BlockSpec((tm, 128), lambda i, j: (j, 0)),
            pl.BlockSpec((half, tm), lambda i, j: (0, j)),
            pl.BlockSpec((half, tm), lambda i, j: (0, j)),
            _const_spec((256, 256)),
        ],
        out_specs=[
            pl.BlockSpec((1, W_DIFF, tm), tr),
            pl.BlockSpec((1, tm, W_DIFF), tok),
            pl.BlockSpec((1, W_DIFF, tm), tr),
            pl.BlockSpec((1, W_NA, tm), tr),
            pl.BlockSpec((1, tm, W_NA), tok),
            pl.BlockSpec((1, W_NA, tm), tr),
        ],
        out_shape=[col_major, row_major, col_major, col_major, row_major, col_major],
        compiler_params=_params(("parallel", "parallel")),
        name="proj",
    )(x, gmix, w_in, gqa, gka, gqn, gkn, cos, sin, cost, sint, gsum)


N_MAPS = 2 * N_HEADS_DIFF
LOG2E = math.log2(math.e)
MAX_STATIC_SHIFT = 60.0
DIFF_KEY_CHUNK = 2048


def _masked_q_group(qt_ref, j, row, lanes=slice(None)):
    grp = qt_ref[0, 128 * (j // 2):128 * (j // 2) + 128, lanes]
    keep = (row >= HEAD_DIM) if (j % 2) else (row < HEAD_DIM)
    return jnp.where(keep, grp, jnp.zeros_like(grp))


def _diff_finish(acc_sc, l_of, lq1_ref, lk1_ref, lq2_ref, lk2_ref, gsub_ref, o_ref, lambda_init):
    lam = (jnp.exp(jnp.sum(lq1_ref[...] * lk1_ref[...])) - jnp.exp(jnp.sum(lq2_ref[...] * lk2_ref[...]))
           + lambda_init)
    for h in range(N_HEADS_DIFF):
        o0 = acc_sc[h] * (1.0 / l_of(h))
        o1 = acc_sc[N_HEADS_DIFF + h] * (lam / l_of(N_HEADS_DIFF + h))
        o = o0 - o1
        o = o * (lax.rsqrt(jnp.mean(o * o, axis=0, keepdims=True) + EPS) * (1.0 - lambda_init)) * gsub_ref[...]
        o_ref[0, :, 128 * h:128 * h + 128] = o.T.astype(o_ref.dtype)


def _diff_online_kernel(shift_ref, qt_ref, k_ref, vt_ref, lq1_ref, lk1_ref, lq2_ref, lk2_ref, gsub_ref, o_ref,
                        qz_sc, m_sc, l_sc, acc_sc, *, lambda_init):
    del shift_ref
    kv = pl.program_id(2)
    tq = qt_ref.shape[2]

    @pl.when(kv == 0)
    def _init():
        m_sc[...] = jnp.full(m_sc.shape, NEG, F32)
        l_sc[...] = jnp.zeros(l_sc.shape, F32)
        acc_sc[...] = jnp.zeros(acc_sc.shape, F32)
        row = lax.broadcasted_iota(jnp.int32, (128, tq), 0)
        for j in range(N_MAPS):
            qz_sc[j] = _masked_q_group(qt_ref, j, row)

    for j in range(N_MAPS):
        h = j % N_HEADS_DIFF
        kp = k_ref[0, :, 128 * (j // 2):128 * (j // 2) + 128]
        s = jnp.dot(kp, qz_sc[j], preferred_element_type=F32)
        m_prev = m_sc[j]
        m_cur = jnp.maximum(m_prev, jnp.max(s, axis=0, keepdims=True))
        alpha = jnp.exp2(m_prev - m_cur)
        p = jnp.exp2(s - m_cur)
        l_sc[j] = alpha * l_sc[j] + jnp.sum(p, axis=0, keepdims=True)
        vt = vt_ref[0, 128 * h:128 * h + 128, :]
        acc_sc[j] = alpha * acc_sc[j] + jnp.dot(vt, p.astype(BF16), preferred_element_type=F32)
        m_sc[j] = m_cur

    @pl.when(kv == pl.num_programs(2) - 1)
    def _finish():
        _diff_finish(acc_sc, lambda j: l_sc[j], lq1_ref, lk1_ref, lq2_ref, lk2_ref, gsub_ref, o_ref, lambda_init)


def _diff_static_kernel(shift_ref, qt_ref, k_ref, vt_ref, lq1_ref, lk1_ref, lq2_ref, lk2_ref, gsub_ref, o_ref,
                        qz_sc, l_sc, acc_sc, *, lambda_init):
    kv = pl.program_id(2)
    tq = qt_ref.shape[2]
    tk = k_ref.shape[1]

    @pl.when(kv == 0)
    def _init():
        l_sc[...] = jnp.zeros(l_sc.shape, F32)
        acc_sc[...] = jnp.zeros(acc_sc.shape, F32)
        row = lax.broadcasted_iota(jnp.int32, (128, tq), 0)
        shift_rows = jnp.where(row == 0, -shift_ref[0], 0.0).astype(BF16)
        for j in range(N_MAPS):
            qz_sc[j, :128, :] = _masked_q_group(qt_ref, j, row)
            qz_sc[j, 128:, :] = shift_rows

    tc = min(DIFF_KEY_CHUNK, tk)
    ones = jnp.ones((tc, 128), BF16)
    units = [(j, c) for j in range(N_MAPS) for c in range(0, tk, tc)]

    def scores(j, c):
        kaug = jnp.concatenate([k_ref[0, c:c + tc, 128 * (j // 2):128 * (j // 2) + 128], ones], axis=1)
        return jnp.dot(kaug, qz_sc[j], preferred_element_type=F32)

    s_next = scores(*units[0])
    for u, (j, c) in enumerate(units):
        h = j % N_HEADS_DIFF
        s = s_next
        if u + 1 < len(units):
            s_next = scores(*units[u + 1])
        p = jnp.exp2(s)
        l_sc[j] += jnp.sum(p.reshape(tc // 8, 8, tq), axis=0)
        vt = vt_ref[0, 128 * h:128 * h + 128, c:c + tc]
        acc_sc[j] += jnp.dot(vt, p.astype(BF16), preferred_element_type=F32)

    @pl.when(kv == pl.num_programs(2) - 1)
    def _finish():
        _diff_finish(acc_sc, lambda j: jnp.sum(l_sc[j], axis=0, keepdims=True),
                     lq1_ref, lk1_ref, lq2_ref, lk2_ref, gsub_ref, o_ref, lambda_init)


def _diff_call(body, scratch, operands, lambda_init, tq, tk):
    qt = operands[1]
    b, _, s = qt.shape
    return pl.pallas_call(
        functools.partial(body, lambda_init=lambda_init),
        grid=(b, s // tq, s // tk),
        in_specs=[
            pl.BlockSpec(memory_space=pltpu.SMEM),
            pl.BlockSpec((1, W_DIFF, tq), lambda i, j, t: (i, 0, j)),
            pl.BlockSpec((1, tk, W_DIFF), lambda i, j, t: (i, t, 0)),
            pl.BlockSpec((1, W_DIFF, tk), lambda i, j, t: (i, 0, t)),
            _const_spec((1, HEAD_DIM)), _const_spec((1, HEAD_DIM)), _const_spec((1, HEAD_DIM)),
            _const_spec((1, HEAD_DIM)),
            _const_spec((2 * HEAD_DIM, 1)),
        ],
        out_specs=pl.BlockSpec((1, tq, W_DIFF), lambda i, j, t: (i, j, 0)),
        out_shape=jax.ShapeDtypeStruct((b, s, W_DIFF), BF16),
        scratch_shapes=scratch,
        compiler_params=_params(("parallel", "parallel", "arbitrary")),
        name=body.__name__.strip("_"),
    )(*operands)


def _diff_attention(shift, qt, k, vt, lq1, lk1, lq2, lk2, gsub, lambda_init, tq, tk):
    operands = (shift, qt, k, vt, lq1, lk1, lq2, lk2, gsub)
    static = functools.partial(
        _diff_call, _diff_static_kernel,
        [pltpu.VMEM((N_MAPS, 256, tq), BF16), pltpu.VMEM((N_MAPS, 8, tq), F32), pltpu.VMEM((N_MAPS, 128, tq), F32)],
        lambda_init=lambda_init, tq=tq, tk=tk)
    online = functools.partial(
        _diff_call, _diff_online_kernel,
        [pltpu.VMEM((N_MAPS, 128, tq), BF16), pltpu.VMEM((N_MAPS, 1, tq), F32), pltpu.VMEM((N_MAPS, 1, tq), F32),
         pltpu.VMEM((N_MAPS, 128, tq), F32)],
        lambda_init=lambda_init, tq=tq, tk=tk)
    return lax.cond(shift[0] <= MAX_STATIC_SHIFT, static, online, operands)


def _na_selectors(rows):
    kh = min(WIN_H, rows)
    assert kh == WIN_H and rows >= NA_KROWS + NA_QROWS
    rsel = np.zeros((3, NA_QROWS, NA_KROWS, 2 * WIN_H - 1), np.float32)
    for v in range(3):
        r0 = (0, NA_QROWS, rows - NA_QROWS)[v]
        ws = min(max(r0 - 4, 0), rows - NA_KROWS)
        for t in range(NA_QROWS):
            r = r0 + t
            rs = min(max(r - kh // 2, 0), rows - kh)
            for i in range(NA_KROWS):
                kr = ws + i
                if rs <= kr < rs + kh:
                    rsel[v, t, i, kr - r + WIN_H - 1] = 1.0
    csel = np.zeros((GRID_W, GRID_W, 2 * WIN_W - 1), np.float32)
    for c in range(GRID_W):
        cs = min(max(c - WIN_W // 2, 0), GRID_W - WIN_W)
        for j in range(cs, cs + WIN_W):
            csel[c, j, j - c + WIN_W - 1] = 1.0
    return rsel, csel


def _na_bias_blocks(na_bias, rows, shifts):
    rsel, csel = _na_selectors(rows)
    depth = na_bias.shape[0]
    n_rel = 2 * WIN_H
    slab = jnp.einsum("lhde,cje->lhdjc", na_bias.astype(F32), csel, precision=lax.Precision.HIGHEST)
    slab = slab * LOG2E - shifts.reshape(-1, 1, 1, 1, 1)
    slab = jnp.where((csel.sum(-1) > 0.5).T, slab, NEG)
    slab = jnp.concatenate([slab, jnp.full_like(slab[:, :, :1], NEG)], axis=2)
    rel_row = np.where(rsel.sum(-1) > 0.5, rsel.argmax(-1), n_rel - 1).astype(np.int32)

    def assemble(rel_ref, slab_ref, o_ref):
        v, i = pl.program_id(1), pl.program_id(2)
        for t in range(NA_QROWS):
            d = rel_ref[(v * NA_QROWS + t) * NA_KROWS + i]
            o_ref[0, 0, :, :, GRID_W * t:GRID_W * (t + 1)] = slab_ref[0, :, d]

    return pl.pallas_call(
        assemble,
        grid_spec=pltpu.PrefetchScalarGridSpec(
            num_scalar_prefetch=1,
            grid=(depth, 3, NA_KROWS),
            in_specs=[pl.BlockSpec((1, N_HEADS_NA, n_rel, GRID_W, GRID_W), lambda l, v, i, rel: (l, 0, 0, 0, 0))],
            out_specs=pl.BlockSpec((1, 1, N_HEADS_NA, GRID_W, NA_QROWS * GRID_W), lambda l, v, i, rel: (l, v, 0, i, 0)),
        ),
        out_shape=jax.ShapeDtypeStruct((depth, 3, N_HEADS_NA, NA_KROWS * GRID_W, NA_QROWS * GRID_W), F32),
        compiler_params=_params(("parallel", "parallel", "parallel")),
        name="natt_bias_blocks",
    )(jnp.asarray(rel_row.reshape(-1)), slab)


def _na_kernel(qt_ref, *refs, subtract_max):
    n = NA_SUBBLOCKS
    k_refs, vt_refs = refs[:n], refs[n:2 * n]
    bias_first, bias_mid, bias_last, o_ref = refs[2 * n:]
    bias_refs = [bias_first] + [bias_mid] * (n - 2) + [bias_last]
    tq = NA_QROWS * GRID_W
    row = lax.broadcasted_iota(jnp.int32, (128, tq), 0)
    units = [(u, h) for u in range(n) for h in range(N_HEADS_NA)]

    def scores(u, h):
        q = _masked_q_group(qt_ref, h, row, slice(u * tq, (u + 1) * tq))
        return jnp.dot(k_refs[u][0, :, 128 * (h // 2):128 * (h // 2) + 128], q,
                       preferred_element_type=F32) + bias_refs[u][0, h]

    outs = []
    s_next = scores(*units[0])
    for i, (u, h) in enumerate(units):
        s = s_next
        if i + 1 < len(units):
            s_next = scores(*units[i + 1])
        if subtract_max:
            s = s - jnp.max(s, axis=0, keepdims=True)
        p = jnp.exp2(s)
        l = jnp.sum(p, axis=0, keepdims=True)
        vt = vt_refs[u][0, HEAD_DIM * h:HEAD_DIM * (h + 1), :]
        outs.append(jnp.dot(vt, p.astype(BF16), preferred_element_type=F32) * (1.0 / l))
        if h == N_HEADS_NA - 1:
            o_ref[0, u * tq:(u + 1) * tq, :] = jnp.concatenate(outs, axis=0).T.astype(o_ref.dtype)
            outs = []


def _na_call(subtract_max, layer, operands):
    qnt, kn, vnt, bias_blocks = operands
    b, s, _ = kn.shape
    rows = s // GRID_W
    nblk = rows // NA_QROWS
    n = NA_SUBBLOCKS
    tq = NA_QROWS * GRID_W
    win = NA_KROWS * GRID_W
    first = lambda g: jnp.clip(g - 1, 0, nblk - 3) * tq
    variant = lambda g: jnp.where(g == 0, 0, jnp.where(g == nblk - 1, 2, 1))
    kblk = lambda u: pl.BlockSpec((pl.Element(1), pl.Element(win), pl.Element(W_NA)),
                                  lambda i, j: (i, first(n * j + u), 0))
    vblk = lambda u: pl.BlockSpec((pl.Element(1), pl.Element(W_NA), pl.Element(win)),
                                  lambda i, j: (i, 0, first(n * j + u)))
    bias_shape = (None, 1, N_HEADS_NA, win, tq)
    bias_specs = [pl.BlockSpec(bias_shape, lambda i, j: (layer, variant(n * j), 0, 0, 0)),
                  _const_spec(bias_shape, (layer, 1, 0, 0, 0)),
                  pl.BlockSpec(bias_shape, lambda i, j: (layer, variant(n * j + n - 1), 0, 0, 0))]
    assert n >= 2 and nblk % n == 0
    subs = range(n)
    return pl.pallas_call(
        functools.partial(_na_kernel, subtract_max=subtract_max),
        grid=(b, nblk // n),
        in_specs=[pl.BlockSpec((1, W_NA, n * tq), lambda i, j: (i, 0, j))]
        + [kblk(u) for u in subs] + [vblk(u) for u in subs] + bias_specs,
        out_specs=pl.BlockSpec((1, n * tq, W_NA), lambda i, j: (i, j, 0)),
        out_shape=jax.ShapeDtypeStruct((b, s, W_NA), BF16),
        compiler_params=_params(("parallel", "arbitrary")),
        name="natt_online" if subtract_max else "natt_static",
    )(qnt, *([kn] * n), *([vnt] * n), bias_blocks, bias_blocks, bias_blocks)


def _na_attention(shift_is_static, layer, qnt, kn, vnt, bias_blocks):
    return lax.cond(shift_is_static, functools.partial(_na_call, False, layer),
                    functools.partial(_na_call, True, layer), (qnt, kn, vnt, bias_blocks))


def _merge_kernel(x_ref, oa_ref, on_ref, gmix_ref, wga_ref, wgb_ref, wpa_ref, wpb_ref, wo_ref, y_ref):
    x = x_ref[0]
    h = _rms(x, gmix_ref[...]).astype(BF16)
    ga = jnp.dot(h, wga_ref[...], preferred_element_type=F32)
    gb = jnp.dot(h, wgb_ref[...], preferred_element_type=F32)
    pa = jnp.dot(oa_ref[0], wpa_ref[...], preferred_element_type=F32)
    pb = jnp.dot(on_ref[0], wpb_ref[...], preferred_element_type=F32)
    mixed = jax.nn.sigmoid(ga) * pa + jax.nn.sigmoid(gb) * pb
    y_ref[0] = x + jnp.dot(mixed.astype(BF16), wo_ref[...], preferred_element_type=F32)


def _merge(x, oa, on, layer, gmix, w_in, w_pa, w_pb, w_o, tm):
    b, s, _ = x.shape
    tok = lambda i, j: (i, j, 0)
    gate_block = ATT_COLS // D_MODEL
    return pl.pallas_call(
        _merge_kernel,
        grid=(b, s // tm),
        in_specs=[
            pl.BlockSpec((1, tm, D_MODEL), tok),
            pl.BlockSpec((1, tm, W_DIFF), tok),
            pl.BlockSpec((1, tm, W_NA), tok),
            _const_spec((1, D_MODEL)),
            _const_spec((None, D_MODEL, D_MODEL), (layer, 0, gate_block)),
            _const_spec((None, D_MODEL, D_MODEL), (layer, 0, gate_block + 1)),
            _const_spec((None, W_DIFF, D_MODEL), (layer, 0, 0)),
            _const_spec((None, W_NA, D_MODEL), (layer, 0, 0)),
            _const_spec((None, D_MODEL, D_MODEL), (layer, 0, 0)),
        ],
        out_specs=pl.BlockSpec((1, tm, D_MODEL), tok),
        out_shape=jax.ShapeDtypeStruct(x.shape, F32),
        compiler_params=_params(("parallel", "parallel")),
        name="merge",
    )(x, oa, on, gmix, w_in, w_in, w_pa, w_pb, w_o)


def _ffn_kernel(x_ref, xp_ref, xn_ref, g_ref, wup_ref, cw_ref, cb_ref, wdn_ref, y_ref):
    j = pl.program_id(1)
    tm = x_ref.shape[1]
    x = x_ref[0]
    xe = jnp.concatenate([xp_ref[0], x, xn_ref[0]], axis=0)
    row = lax.broadcasted_iota(jnp.int32, (tm + 2 * HALO, 1), 0)
    inside = jnp.logical_and(jnp.logical_or(row >= HALO, j > 0),
                             jnp.logical_or(row < tm + HALO, j < pl.num_programs(1) - 1))
    he = jnp.where(inside, _rms(xe, g_ref[...]), 0.0).astype(BF16)

    def up(c0, n):
        return tuple(jnp.dot(he, wup_ref[:, c:c + n], preferred_element_type=F32)
                     for c in (c0, D_FF + c0))

    def conv(u, c0, n):
        w = cw_ref[:, c0:c0 + n]
        prev = pltpu.roll(u, 1, 0)[HALO:HALO + tm]
        nxt = pltpu.roll(u, tm + 2 * HALO - 1, 0)[HALO:HALO + tm]
        return prev * w[0:1] + u[HALO:HALO + tm] * w[1:2] + nxt * w[2:3] + cb_ref[:, c0:c0 + n]

    def gelu(v):
        c = math.sqrt(2.0 / math.pi)
        half_v = 0.5 * v
        return half_v + half_v * jnp.tanh(v * (c + (c * 0.044715) * (v * v)))

    starts = [sum(FFN_CHUNKS[:i]) for i in range(len(FFN_CHUNKS))]
    acc = x
    u_next = up(starts[0], FFN_CHUNKS[0])
    for i, (c0, n) in enumerate(zip(starts, FFN_CHUNKS)):
        ug, uv = u_next
        if i + 1 < len(FFN_CHUNKS):
            u_next = up(starts[i + 1], FFN_CHUNKS[i + 1])
        act = (gelu(conv(ug, c0, n)) * conv(uv, D_FF + c0, n)).astype(BF16)
        acc = acc + jnp.dot(act, wdn_ref[c0:c0 + n, :], preferred_element_type=F32)
    y_ref[0] = acc


def _ffn(x, layer, g, w_up, conv_w, conv_b, w_down, tm):
    b, s, _ = x.shape
    nb = tm // HALO
    last = s // HALO - 1
    return pl.pallas_call(
        _ffn_kernel,
        grid=(b, s // tm),
        in_specs=[
            pl.BlockSpec((1, tm, D_MODEL), lambda i, j: (i, j, 0)),
            pl.BlockSpec((1, HALO, D_MODEL), lambda i, j: (i, jnp.maximum(j * nb - 1, 0), 0)),
            pl.BlockSpec((1, HALO, D_MODEL), lambda i, j: (i, jnp.minimum((j + 1) * nb, last), 0)),
            _const_spec((1, D_MODEL)),
            _const_spec((None, D_MODEL, 2 * D_FF), (layer, 0, 0)),
            _const_spec((None, 3, 2 * D_FF), (layer, 0, 0)),
            _const_spec((None, 1, 2 * D_FF), (layer, 0, 0)),
            _const_spec((None, D_FF, D_MODEL), (layer, 0, 0)),
        ],
        out_specs=pl.BlockSpec((1, tm, D_MODEL), lambda i, j: (i, j, 0)),
        out_shape=jax.ShapeDtypeStruct(x.shape, F32),
        compiler_params=_params(("parallel", "parallel")),
        name="ffn",
    )(x, x, x, g, w_up, conv_w, conv_b, w_down)


def _rope_tables(seq):
    half = HEAD_DIM // 2
    inv = jnp.power(ROPE_THETA, -jnp.arange(half, dtype=F32) * 2.0 / HEAD_DIM)
    ang = jnp.arange(seq, dtype=F32)[:, None] * inv[None, :]
    cos, sin = jnp.cos(ang), jnp.sin(ang)
    return jnp.tile(cos, (1, 4)), jnp.tile(jnp.concatenate([-sin, sin], axis=-1), (1, 2)), cos.T, sin.T


def _group_sum_matrix():
    idx = np.arange(256) // HEAD_DIM
    return jnp.asarray(idx[:, None] == idx[None, :], dtype=BF16)


def _tiles(seq):
    tm = min(1024, seq)
    tm_ffn = min(512, seq)
    tq = min(1024, seq)
    tk = min(2048, seq)
    return tm, tm_ffn, tq, tk


def _trunk(x, weights, na_bias, na_shift):
    stacked, per_layer = weights
    seq = x.shape[1]
    tm, tm_ffn, tq, tk = _tiles(seq)
    cos, sin, cost, sint = _rope_tables(seq)
    gsum = _group_sum_matrix()
    na_static = na_shift <= MAX_STATIC_SHIFT
    bias_blocks = _na_bias_blocks(na_bias, seq // GRID_W, jnp.where(na_static, na_shift, 0.0))
    for l, w in enumerate(per_layer):
        lambda_init = 0.8 - 0.6 * math.exp(-0.3 * l)
        qt, k, vt, qnt, kn, vnt = _proj(x, l, w["g_mix"], stacked["w_in"], w["gq_a"], w["gk_a"], w["gq_n"],
                                        w["gk_n"], cos, sin, cost, sint, gsum, tm)
        oa = _diff_attention(w["shift"], qt, k, vt, w["lam_q1"], w["lam_k1"], w["lam_q2"], w["lam_k2"], w["g_sub"],
                             lambda_init, tq, tk)
        on = _na_attention(na_static[l], l, qnt, kn, vnt, bias_blocks)
        x = _merge(x, oa, on, l, w["g_mix"], stacked["w_in"], stacked["w_pa"], stacked["w_pb"], stacked["w_o"], tm)
        x = _ffn(x, l, w["g_ffn"], stacked["w_up"], stacked["conv_w"], stacked["conv_b"], stacked["w_down"],
                 tm_ffn)
    return x


def _score_bound(gq, gk):
    bound = (HEAD_DIM ** 0.5) * LOG2E * jnp.max(jnp.abs(gq)) * jnp.max(jnp.abs(gk))
    return jnp.ceil(1.02 * bound.astype(F32)).reshape(1)


def _na_score_bound(gq, gk, na_bias):
    amax = lambda a: jnp.max(jnp.abs(a.astype(F32)).reshape(a.shape[0], -1), axis=1)
    bound = LOG2E * ((HEAD_DIM ** 0.5) * amax(gq) * amax(gk) + amax(na_bias))
    return jnp.ceil(1.02 * bound)


def _layer_weights(g_mix, w_in, gq_a, gk_a, lam_q1, lam_k1, lam_q2, lam_k2, g_sub, gq_n, gk_n,
                   w_pa, w_pb, w_o, g_ffn, w_up, conv_w, conv_b, w_down):
    depth = w_in.shape[0]
    stacked = dict(w_in=w_in.astype(BF16), w_pa=w_pa.astype(BF16), w_pb=w_pb.astype(BF16), w_o=w_o.astype(BF16),
                   w_up=w_up.astype(BF16), w_down=w_down.astype(BF16), conv_w=conv_w.astype(F32),
                   conv_b=conv_b.astype(F32)[:, None, :])
    head_tile = lambda g: jnp.tile(g.astype(F32), W_DIFF // HEAD_DIM)[None, :]
    row = lambda g: g.astype(F32)[None, :]
    col = lambda g: g.astype(F32)[:, None]
    per_layer = []
    for l in range(depth):
        per_layer.append(dict(
            g_mix=row(g_mix[l]), gq_a=col(gq_a[l]), gk_a=head_tile(gk_a[l]), gq_n=col(gq_n[l]),
            gk_n=head_tile(gk_n[l]),
            lam_q1=row(lam_q1[l]), lam_k1=row(lam_k1[l]), lam_q2=row(lam_q2[l]), lam_k2=row(lam_k2[l]),
            g_sub=col(g_sub[l]), shift=_score_bound(gq_a[l], gk_a[l]), g_ffn=row(g_ffn[l])))
    return stacked, per_layer


def kernel(x_prompt, x_sample, g_mix, w_in, gq_a, gk_a, lam_q1, lam_k1, lam_q2, lam_k2, g_sub, gq_n, gk_n,
           na_bias, w_pa, w_pb, w_o, g_ffn, w_up, conv_w, conv_b, w_down):
    weights = _layer_weights(g_mix, w_in, gq_a, gk_a, lam_q1, lam_k1, lam_q2, lam_k2, g_sub, gq_n, gk_n,
                             w_pa, w_pb, w_o, g_ffn, w_up, conv_w, conv_b, w_down)
    na_shift = _na_score_bound(gq_n, gk_n, na_bias)
    return tuple(_trunk(x, weights, na_bias, na_shift) for x in (x_prompt, x_sample))
```

```python
import functools
import math

import numpy as np
import jax
import jax.numpy as jnp
from jax import lax
from jax.experimental import pallas as pl
from jax.experimental.pallas import tpu as pltpu

D_MODEL = 1024
HEAD_DIM = 64
N_HEADS_DIFF = 4
N_HEADS_NA = 8
W_DIFF = N_HEADS_DIFF * 2 * HEAD_DIM
W_NA = N_HEADS_NA * HEAD_DIM
ATT_COLS = 3 * W_DIFF + 3 * W_NA
GRID_W = 64
WIN_H = 8
WIN_W = 16
D_FF = 2816
ROPE_THETA = 10000.0
EPS = 1e-6
NEG = -1e30

NA_QROWS = 4
NA_KROWS = 12
NA_SUBBLOCKS = 4
NA_LOOKAHEAD = 2
FFN_CHUNKS = (2304, 512)
HALO = 8

VMEM_LIMIT = 56 * 1024 * 1024

F32 = jnp.float32
BF16 = jnp.bfloat16


def _const_spec(shape, index=None):
    index = (0,) * len(shape) if index is None else tuple(index)
    return pl.BlockSpec(shape, lambda *_: index, pipeline_mode=pl.Buffered(1))


def _params(sem):
    return pltpu.CompilerParams(dimension_semantics=sem, vmem_limit_bytes=VMEM_LIMIT)


def _rms(x, g):
    return x * lax.rsqrt(jnp.mean(x * x, axis=-1, keepdims=True) + EPS) * g


def _proj_kernel(x_ref, gmix_ref, w_ref, gqa_ref, gka_ref, gqn_ref, gkn_ref, cos_ref, sin_ref, cost_ref, sint_ref,
                 gsum_ref, qt_ref, k_ref, vt_ref, qnt_ref, kn_ref, vnt_ref):
    x = x_ref[0]
    h = _rms(x, gmix_ref[...]).astype(BF16)
    gsum = gsum_ref[...]
    half = HEAD_DIM // 2

    def proj(c0):
        return jnp.dot(h, w_ref[:, c0:c0 + W_DIFF], preferred_element_type=F32)

    def head_norm_t(zt, gcol):
        heads = []
        for r in range(0, W_DIFF, HEAD_DIM):
            blk = zt[r:r + HEAD_DIM]
            heads.append(blk * lax.rsqrt(jnp.mean(blk * blk, axis=0, keepdims=True) + EPS) * gcol)
        return heads

    def rope_t(blk):
        x1, x2 = blk[:half], blk[half:]
        return jnp.concatenate([x1 * cost_ref[...] - x2 * sint_ref[...], x1 * sint_ref[...] + x2 * cost_ref[...]],
                               axis=0)

    def head_norm(z, g):
        z2 = z * z
        hi = z2.astype(BF16)
        lo = (z2 - hi.astype(F32)).astype(BF16)
        parts = []
        for c in range(0, W_DIFF, 256):
            parts.append(jnp.dot(hi[:, c:c + 256], gsum, preferred_element_type=F32)
                         + jnp.dot(lo[:, c:c + 256], gsum, preferred_element_type=F32))
        ss = jnp.concatenate(parts, axis=-1)
        return z * lax.rsqrt(ss * (1.0 / HEAD_DIM) + EPS) * g

    cos = jnp.concatenate([cos_ref[...]] * 4, axis=-1)
    sin = jnp.concatenate([sin_ref[...]] * 4, axis=-1)
    lane = lax.broadcasted_iota(jnp.int32, (1, W_DIFF), 1)
    first_half = (lane % HEAD_DIM) < (HEAD_DIM // 2)

    def rope(y):
        partner = jnp.where(first_half, pltpu.roll(y, W_DIFF - HEAD_DIM // 2, 1), pltpu.roll(y, HEAD_DIM // 2, 1))
        return y * cos + partner * sin

    scale = HEAD_DIM ** -0.5 * LOG2E

    def emit_qa(z):
        qa = [rope_t(blk) for blk in head_norm_t(z.T, gqa_ref[...])]
        qt_ref[0] = (jnp.concatenate(qa, axis=0) * scale).astype(BF16)

    def emit_ka(z):
        k_ref[0] = rope(head_norm(z, gka_ref[...])).astype(BF16)

    def emit_va(z):
        vt_ref[0] = z.T.astype(BF16)

    def emit_qn(z):
        qnt_ref[0] = (jnp.concatenate(head_norm_t(z.T, gqn_ref[...]), axis=0) * scale).astype(BF16)

    def emit_kn(z):
        kn_ref[0] = head_norm(z, gkn_ref[...]).astype(BF16)

    def emit_vn(z):
        vnt_ref[0] = z.T.astype(BF16)

    emitters = (emit_qa, emit_ka, emit_va, emit_qn, emit_kn, emit_vn)
    z_next = proj(0)
    for i, emit in enumerate(emitters):
        z = z_next
        if i + 1 < len(emitters):
            z_next = proj((i + 1) * W_DIFF)
        emit(z)


def _proj(x, layer, gmix, w_in, gqa, gka, gqn, gkn, cos, sin, cost, sint, gsum, tm):
    b, s, _ = x.shape
    tok = lambda i, j: (i, j, 0)
    tr = lambda i, j: (i, 0, j)
    row_major = jax.ShapeDtypeStruct((b, s, W_DIFF), BF16)
    col_major = jax.ShapeDtypeStruct((b, W_DIFF, s), BF16)
    half = HEAD_DIM // 2
    return pl.pallas_call(
        _proj_kernel,
        grid=(b, s // tm),
        in_specs=[
            pl.BlockSpec((1, tm, D_MODEL), tok),
            _const_spec((1, D_MODEL)),
            _const_spec((None, D_MODEL, ATT_COLS), (layer, 0, 0)),
            _const_spec((HEAD_DIM, 1)), _const_spec((1, W_DIFF)), _const_spec((HEAD_DIM, 1)), _const_spec((1, W_NA)),
            pl.BlockSpec((tm, 128), lambda i, j: (j, 0)),
            pl.BlockSpec((tm, 128), lambda i, j: (j, 0)),
            pl.BlockSpec((half, tm), lambda i, j: (0, j)),
            pl.BlockSpec((half, tm), lambda i, j: (0, j)),
            _const_spec((256, 256)),
        ],
        out_specs=[
            pl.BlockSpec((1, W_DIFF, tm), tr),
            pl.BlockSpec((1, tm, W_DIFF), tok),
            pl.BlockSpec((1, W_DIFF, tm), tr),
            pl.BlockSpec((1, W_NA, tm), tr),
            pl.BlockSpec((1, tm, W_NA), tok),
            pl.BlockSpec((1, W_NA, tm), tr),
        ],
        out_shape=[col_major, row_major, col_major, col_major, row_major, col_major],
        compiler_params=_params(("parallel", "parallel")),
        name="proj",
    )(x, gmix, w_in, gqa, gka, gqn, gkn, cos, sin, cost, sint, gsum)


N_MAPS = 2 * N_HEADS_DIFF
LOG2E = math.log2(math.e)
MAX_STATIC_SHIFT = 60.0
DIFF_KEY_CHUNK = 2048


def _masked_q_group(qt_ref, j, row, lanes=slice(None)):
    grp = qt_ref[0, 128 * (j // 2):128 * (j // 2) + 128, lanes]
    keep = (row >= HEAD_DIM) if (j % 2) else (row < HEAD_DIM)
    return jnp.where(keep, grp, jnp.zeros_like(grp))


def _diff_finish(acc_sc, l_of, lq1_ref, lk1_ref, lq2_ref, lk2_ref, gsub_ref, o_ref, lambda_init):
    lam = (jnp.exp(jnp.sum(lq1_ref[...] * lk1_ref[...])) - jnp.exp(jnp.sum(lq2_ref[...] * lk2_ref[...]))
           + lambda_init)
    for h in range(N_HEADS_DIFF):
        o0 = acc_sc[h] * (1.0 / l_of(h))
        o1 = acc_sc[N_HEADS_DIFF + h] * (lam / l_of(N_HEADS_DIFF + h))
        o = o0 - o1
        o = o * (lax.rsqrt(jnp.mean(o * o, axis=0, keepdims=True) + EPS) * (1.0 - lambda_init)) * gsub_ref[...]
        o_ref[0, :, 128 * h:128 * h + 128] = o.T.astype(o_ref.dtype)


def _diff_online_kernel(shift_ref, qt_ref, k_ref, vt_ref, lq1_ref, lk1_ref, lq2_ref, lk2_ref, gsub_ref, o_ref,
                        qz_sc, m_sc, l_sc, acc_sc, *, lambda_init):
    del shift_ref
    kv = pl.program_id(2)
    tq = qt_ref.shape[2]

    @pl.when(kv == 0)
    def _init():
        m_sc[...] = jnp.full(m_sc.shape, NEG, F32)
        l_sc[...] = jnp.zeros(l_sc.shape, F32)
        acc_sc[...] = jnp.zeros(acc_sc.shape, F32)
        row = lax.broadcasted_iota(jnp.int32, (128, tq), 0)
        for j in range(N_MAPS):
            qz_sc[j] = _masked_q_group(qt_ref, j, row)

    for j in range(N_MAPS):
        h = j % N_HEADS_DIFF
        kp = k_ref[0, :, 128 * (j // 2):128 * (j // 2) + 128]
        s = jnp.dot(kp, qz_sc[j], preferred_element_type=F32)
        m_prev = m_sc[j]
        m_cur = jnp.maximum(m_prev, jnp.max(s, axis=0, keepdims=True))
        alpha = jnp.exp2(m_prev - m_cur)
        p = jnp.exp2(s - m_cur)
        l_sc[j] = alpha * l_sc[j] + jnp.sum(p, axis=0, keepdims=True)
        vt = vt_ref[0, 128 * h:128 * h + 128, :]
        acc_sc[j] = alpha * acc_sc[j] + jnp.dot(vt, p.astype(BF16), preferred_element_type=F32)
        m_sc[j] = m_cur

    @pl.when(kv == pl.num_programs(2) - 1)
    def _finish():
        _diff_finish(acc_sc, lambda j: l_sc[j], lq1_ref, lk1_ref, lq2_ref, lk2_ref, gsub_ref, o_ref, lambda_init)


def _diff_static_kernel(shift_ref, qt_ref, k_ref, vt_ref, lq1_ref, lk1_ref, lq2_ref, lk2_ref, gsub_ref, o_ref,
                        qz_sc, l_sc, acc_sc, *, lambda_init):
    kv = pl.program_id(2)
    tq = qt_ref.shape[2]
    tk = k_ref.shape[1]

    @pl.when(kv == 0)
    def _init():
        l_sc[...] = jnp.zeros(l_sc.shape, F32)
        acc_sc[...] = jnp.zeros(acc_sc.shape, F32)
        row = lax.broadcasted_iota(jnp.int32, (128, tq), 0)
        shift_rows = jnp.where(row == 0, -shift_ref[0], 0.0).astype(BF16)
        for j in range(N_MAPS):
            qz_sc[j, :128, :] = _masked_q_group(qt_ref, j, row)
            qz_sc[j, 128:, :] = shift_rows

    tc = min(DIFF_KEY_CHUNK, tk)
    ones = jnp.ones((tc, 128), BF16)
    units = [(j, c) for j in range(N_MAPS) for c in range(0, tk, tc)]

    def scores(j, c):
        kaug = jnp.concatenate([k_ref[0, c:c + tc, 128 * (j // 2):128 * (j // 2) + 128], ones], axis=1)
        return jnp.dot(kaug, qz_sc[j], preferred_element_type=F32)

    s_next = scores(*units[0])
    for u, (j, c) in enumerate(units):
        h = j % N_HEADS_DIFF
        s = s_next
        if u + 1 < len(units):
            s_next = scores(*units[u + 1])
        p = jnp.exp2(s)
        l_sc[j] += jnp.sum(p.reshape(tc // 8, 8, tq), axis=0)
        vt = vt_ref[0, 128 * h:128 * h + 128, c:c + tc]
        acc_sc[j] += jnp.dot(vt, p.astype(BF16), preferred_element_type=F32)

    @pl.when(kv == pl.num_programs(2) - 1)
    def _finish():
        _diff_finish(acc_sc, lambda j: jnp.sum(l_sc[j], axis=0, keepdims=True),
                     lq1_ref, lk1_ref, lq2_ref, lk2_ref, gsub_ref, o_ref, lambda_init)


def _diff_call(body, scratch, operands, lambda_init, tq, tk):
    qt = operands[1]
    b, _, s = qt.shape
    return pl.pallas_call(
        functools.partial(body, lambda_init=lambda_init),
        grid=(b, s // tq, s // tk),
        in_specs=[
            pl.BlockSpec(memory_space=pltpu.SMEM),
            pl.BlockSpec((1, W_DIFF, tq), lambda i, j, t: (i, 0, j)),
            pl.BlockSpec((1, tk, W_DIFF), lambda i, j, t: (i, t, 0)),
            pl.BlockSpec((1, W_DIFF, tk), lambda i, j, t: (i, 0, t)),
            _const_spec((1, HEAD_DIM)), _const_spec((1, HEAD_DIM)), _const_spec((1, HEAD_DIM)),
            _const_spec((1, HEAD_DIM)),
            _const_spec((2 * HEAD_DIM, 1)),
        ],
        out_specs=pl.BlockSpec((1, tq, W_DIFF), lambda i, j, t: (i, j, 0)),
        out_shape=jax.ShapeDtypeStruct((b, s, W_DIFF), BF16),
        scratch_shapes=scratch,
        compiler_params=_params(("parallel", "parallel", "arbitrary")),
        name=body.__name__.strip("_"),
    )(*operands)


def _diff_attention(shift, qt, k, vt, lq1, lk1, lq2, lk2, gsub, lambda_init, tq, tk):
    operands = (shift, qt, k, vt, lq1, lk1, lq2, lk2, gsub)
    static = functools.partial(
        _diff_call, _diff_static_kernel,
        [pltpu.VMEM((N_MAPS, 256, tq), BF16), pltpu.VMEM((N_MAPS, 8, tq), F32), pltpu.VMEM((N_MAPS, 128, tq), F32)],
        lambda_init=lambda_init, tq=tq, tk=tk)
    online = functools.partial(
        _diff_call, _diff_online_kernel,
        [pltpu.VMEM((N_MAPS, 128, tq), BF16), pltpu.VMEM((N_MAPS, 1, tq), F32), pltpu.VMEM((N_MAPS, 1, tq), F32),
         pltpu.VMEM((N_MAPS, 128, tq), F32)],
        lambda_init=lambda_init, tq=tq, tk=tk)
    return lax.cond(shift[0] <= MAX_STATIC_SHIFT, static, online, operands)


def _na_selectors(rows):
    kh = min(WIN_H, rows)
    assert kh == WIN_H and rows >= NA_KROWS + NA_QROWS
    rsel = np.zeros((3, NA_QROWS, NA_KROWS, 2 * WIN_H - 1), np.float32)
    for v in range(3):
        r0 = (0, NA_QROWS, rows - NA_QROWS)[v]
        ws = min(max(r0 - 4, 0), rows - NA_KROWS)
        for t in range(NA_QROWS):
            r = r0 + t
            rs = min(max(r - kh // 2, 0), rows - kh)
            for i in range(NA_KROWS):
                kr = ws + i
                if rs <= kr < rs + kh:
                    rsel[v, t, i, kr - r + WIN_H - 1] = 1.0
    csel = np.zeros((GRID_W, GRID_W, 2 * WIN_W - 1), np.float32)
    for c in range(GRID_W):
        cs = min(max(c - WIN_W // 2, 0), GRID_W - WIN_W)
        for j in range(cs, cs + WIN_W):
            csel[c, j, j - c + WIN_W - 1] = 1.0
    return rsel, csel


def _na_bias_blocks(na_bias, rows, shifts):
    rsel, csel = _na_selectors(rows)
    depth = na_bias.shape[0]
    n_rel = 2 * WIN_H
    slab = jnp.einsum("lhde,cje->lhdjc", na_bias.astype(F32), csel, precision=lax.Precision.HIGHEST)
    slab = slab * LOG2E - shifts.reshape(-1, 1, 1, 1, 1)
    slab = jnp.where((csel.sum(-1) > 0.5).T, slab, NEG)
    slab = jnp.concatenate([slab, jnp.full_like(slab[:, :, :1], NEG)], axis=2)
    rel_row = np.where(rsel.sum(-1) > 0.5, rsel.argmax(-1), n_rel - 1).astype(np.int32)

    per_step = 4

    def assemble(rel_ref, slab_ref, o_ref):
        v, i0 = pl.program_id(1), pl.program_id(2) * per_step
        for r in range(per_step):
            for t in range(NA_QROWS):
                d = rel_ref[(v * NA_QROWS + t) * NA_KROWS + i0 + r]
                o_ref[0, 0, :, GRID_W * r:GRID_W * (r + 1), GRID_W * t:GRID_W * (t + 1)] = slab_ref[0, :, d]

    return pl.pallas_call(
        assemble,
        grid_spec=pltpu.PrefetchScalarGridSpec(
            num_scalar_prefetch=1,
            grid=(depth, 3, NA_KROWS // per_step),
            in_specs=[pl.BlockSpec((1, N_HEADS_NA, n_rel, GRID_W, GRID_W), lambda l, v, i, rel: (l, 0, 0, 0, 0))],
            out_specs=pl.BlockSpec((1, 1, N_HEADS_NA, per_step * GRID_W, NA_QROWS * GRID_W),
                                   lambda l, v, i, rel: (l, v, 0, i, 0)),
        ),
        out_shape=jax.ShapeDtypeStruct((depth, 3, N_HEADS_NA, NA_KROWS * GRID_W, NA_QROWS * GRID_W), F32),
        compiler_params=_params(("parallel", "parallel", "parallel")),
        name="natt_bias_blocks",
    )(jnp.asarray(rel_row.reshape(-1)), slab)


def _na_kernel(qt_ref, *refs, subtract_max):
    n = NA_SUBBLOCKS
    k_refs, vt_refs = refs[:n], refs[n:2 * n]
    bias_first, bias_mid, bias_last, o_ref = refs[2 * n:]
    bias_refs = [bias_first] + [bias_mid] * (n - 2) + [bias_last]
    tq = NA_QROWS * GRID_W
    row = lax.broadcasted_iota(jnp.int32, (128, tq), 0)
    units = [(u, h) for u in range(n) for h in range(N_HEADS_NA)]

    def scores(u, h):
        q = _masked_q_group(qt_ref, h, row, slice(u * tq, (u + 1) * tq))
        return jnp.dot(k_refs[u][0, :, 128 * (h // 2):128 * (h // 2) + 128], q,
                       preferred_element_type=F32) + bias_refs[u][0, h]

    outs = []
    pending = [scores(*unit) for unit in units[:NA_LOOKAHEAD]]
    for i, (u, h) in enumerate(units):
        s = pending.pop(0)
        if i + NA_LOOKAHEAD < len(units):
            pending.append(scores(*units[i + NA_LOOKAHEAD]))
        if subtract_max:
            s = s - jnp.max(s, axis=0, keepdims=True)
        p = jnp.exp2(s)
        l = jnp.sum(p, axis=0, keepdims=True)
        vt = vt_refs[u][0, HEAD_DIM * h:HEAD_DIM * (h + 1), :]
        outs.append(jnp.dot(vt, p.astype(BF16), preferred_element_type=F32) * (1.0 / l))
        if h == N_HEADS_NA - 1:
            o_ref[0, u * tq:(u + 1) * tq, :] = jnp.concatenate(outs, axis=0).T.astype(o_ref.dtype)
            outs = []


def _na_call(subtract_max, layer, operands):
    qnt, kn, vnt, bias_blocks = operands
    b, s, _ = kn.shape
    rows = s // GRID_W
    nblk = rows // NA_QROWS
    n = NA_SUBBLOCKS
    tq = NA_QROWS * GRID_W
    win = NA_KROWS * GRID_W
    first = lambda g: jnp.clip(g - 1, 0, nblk - 3) * tq
    variant = lambda g: jnp.where(g == 0, 0, jnp.where(g == nblk - 1, 2, 1))
    kblk = lambda u: pl.BlockSpec((pl.Element(1), pl.Element(win), pl.Element(W_NA)),
                                  lambda i, j: (i, first(n * j + u), 0))
    vblk = lambda u: pl.BlockSpec((pl.Element(1), pl.Element(W_NA), pl.Element(win)),
                                  lambda i, j: (i, 0, first(n * j + u)))
    bias_shape = (None, 1, N_HEADS_NA, win, tq)
    bias_specs = [pl.BlockSpec(bias_shape, lambda i, j: (layer, variant(n * j), 0, 0, 0)),
                  _const_spec(bias_shape, (layer, 1, 0, 0, 0)),
                  pl.BlockSpec(bias_shape, lambda i, j: (layer, variant(n * j + n - 1), 0, 0, 0))]
    assert n >= 2 and nblk % n == 0
    subs = range(n)
    return pl.pallas_call(
        functools.partial(_na_kernel, subtract_max=subtract_max),
        grid=(b, nblk // n),
        in_specs=[pl.BlockSpec((1, W_NA, n * tq), lambda i, j: (i, 0, j))]
        + [kblk(u) for u in subs] + [vblk(u) for u in subs] + bias_specs,
        out_specs=pl.BlockSpec((1, n * tq, W_NA), lambda i, j: (i, j, 0)),
        out_shape=jax.ShapeDtypeStruct((b, s, W_NA), BF16),
        compiler_params=_params(("parallel", "arbitrary")),
        name="natt_online" if subtract_max else "natt_static",
    )(qnt, *([kn] * n), *([vnt] * n), bias_blocks, bias_blocks, bias_blocks)


def _na_attention(shift_is_static, layer, qnt, kn, vnt, bias_blocks):
    return lax.cond(shift_is_static, functools.partial(_na_call, False, layer),
                    functools.partial(_na_call, True, layer), (qnt, kn, vnt, bias_blocks))


def _merge_kernel(x_ref, oa_ref, on_ref, gmix_ref, wga_ref, wgb_ref, wpa_ref, wpb_ref, wo_ref, y_ref):
    x = x_ref[0]
    h = _rms(x, gmix_ref[...]).astype(BF16)
    ga = jnp.dot(h, wga_ref[...], preferred_element_type=F32)
    gb = jnp.dot(h, wgb_ref[...], preferred_element_type=F32)
    pa = jnp.dot(oa_ref[0], wpa_ref[...], preferred_element_type=F32)
    pb = jnp.dot(on_ref[0], wpb_ref[...], preferred_element_type=F32)
    mixed = jax.nn.sigmoid(ga) * pa + jax.nn.sigmoid(gb) * pb
    y_ref[0] = x + jnp.dot(mixed.astype(BF16), wo_ref[...], preferred_element_type=F32)


def _merge(x, oa, on, layer, gmix, w_in, w_pa, w_pb, w_o, tm):
    b, s, _ = x.shape
    tok = lambda i, j: (i, j, 0)
    gate_block = ATT_COLS // D_MODEL
    return pl.pallas_call(
        _merge_kernel,
        grid=(b, s // tm),
        in_specs=[
            pl.BlockSpec((1, tm, D_MODEL), tok),
            pl.BlockSpec((1, tm, W_DIFF), tok),
            pl.BlockSpec((1, tm, W_NA), tok),
            _const_spec((1, D_MODEL)),
            _const_spec((None, D_MODEL, D_MODEL), (layer, 0, gate_block)),
            _const_spec((None, D_MODEL, D_MODEL), (layer, 0, gate_block + 1)),
            _const_spec((None, W_DIFF, D_MODEL), (layer, 0, 0)),
            _const_spec((None, W_NA, D_MODEL), (layer, 0, 0)),
            _const_spec((None, D_MODEL, D_MODEL), (layer, 0, 0)),
        ],
        out_specs=pl.BlockSpec((1, tm, D_MODEL), tok),
        out_shape=jax.ShapeDtypeStruct(x.shape, F32),
        compiler_params=_params(("parallel", "parallel")),
        name="merge",
    )(x, oa, on, gmix, w_in, w_in, w_pa, w_pb, w_o)


def _ffn_kernel(x_ref, xp_ref, xn_ref, g_ref, wup_ref, cw_ref, cb_ref, wdn_ref, y_ref):
    j = pl.program_id(1)
    tm = x_ref.shape[1]
    x = x_ref[0]
    xe = jnp.concatenate([xp_ref[0], x, xn_ref[0]], axis=0)
    row = lax.broadcasted_iota(jnp.int32, (tm + 2 * HALO, 1), 0)
    inside = jnp.logical_and(jnp.logical_or(row >= HALO, j > 0),
                             jnp.logical_or(row < tm + HALO, j < pl.num_programs(1) - 1))
    he = jnp.where(inside, _rms(xe, g_ref[...]), 0.0).astype(BF16)

    def up(c0, n):
        return tuple(jnp.dot(he, wup_ref[:, c:c + n], preferred_element_type=F32)
                     for c in (c0, D_FF + c0))

    def conv(u, c0, n):
        w = cw_ref[:, c0:c0 + n]
        prev = pltpu.roll(u, 1, 0)[HALO:HALO + tm]
        nxt = pltpu.roll(u, tm + 2 * HALO - 1, 0)[HALO:HALO + tm]
        return prev * w[0:1] + u[HALO:HALO + tm] * w[1:2] + nxt * w[2:3] + cb_ref[:, c0:c0 + n]

    def gelu(v):
        c = math.sqrt(2.0 / math.pi)
        half_v = 0.5 * v
        return half_v + half_v * jnp.tanh(v * (c + (c * 0.044715) * (v * v)))

    starts = [sum(FFN_CHUNKS[:i]) for i in range(len(FFN_CHUNKS))]
    acc = x
    u_next = up(starts[0], FFN_CHUNKS[0])
    for i, (c0, n) in enumerate(zip(starts, FFN_CHUNKS)):
        ug, uv = u_next
        if i + 1 < len(FFN_CHUNKS):
            u_next = up(starts[i + 1], FFN_CHUNKS[i + 1])
        act = (gelu(conv(ug, c0, n)) * conv(uv, D_FF + c0, n)).astype(BF16)
        acc = acc + jnp.dot(act, wdn_ref[c0:c0 + n, :], preferred_element_type=F32)
    y_ref[0] = acc


def _ffn(x, layer, g, w_up, conv_w, conv_b, w_down, tm):
    b, s, _ = x.shape
    nb = tm // HALO
    last = s // HALO - 1
    return pl.pallas_call(
        _ffn_kernel,
        grid=(b, s // tm),
        in_specs=[
            pl.BlockSpec((1, tm, D_MODEL), lambda i, j: (i, j, 0)),
            pl.BlockSpec((1, HALO, D_MODEL), lambda i, j: (i, jnp.maximum(j * nb - 1, 0), 0)),
            pl.BlockSpec((1, HALO, D_MODEL), lambda i, j: (i, jnp.minimum((j + 1) * nb, last), 0)),
            _const_spec((1, D_MODEL)),
            _const_spec((None, D_MODEL, 2 * D_FF), (layer, 0, 0)),
            _const_spec((None, 3, 2 * D_FF), (layer, 0, 0)),
            _const_spec((None, 1, 2 * D_FF), (layer, 0, 0)),
            _const_spec((None, D_FF, D_MODEL), (layer, 0, 0)),
        ],
        out_specs=pl.BlockSpec((1, tm, D_MODEL), lambda i, j: (i, j, 0)),
        out_shape=jax.ShapeDtypeStruct(x.shape, F32),
        compiler_params=_params(("parallel", "parallel")),
        name="ffn",
    )(x, x, x, g, w_up, conv_w, conv_b, w_down)


def _rope_tables(seq):
    half = HEAD_DIM // 2
    inv = jnp.power(ROPE_THETA, -jnp.arange(half, dtype=F32) * 2.0 / HEAD_DIM)
    ang = jnp.arange(seq, dtype=F32)[:, None] * inv[None, :]
    cos, sin = jnp.cos(ang), jnp.sin(ang)
    return jnp.tile(cos, (1, 4)), jnp.tile(jnp.concatenate([-sin, sin], axis=-1), (1, 2)), cos.T, sin.T


def _group_sum_matrix():
    idx = np.arange(256) // HEAD_DIM
    return jnp.asarray(idx[:, None] == idx[None, :], dtype=BF16)


def _tiles(seq):
    tm = min(1024, seq)
    tm_ffn = min(512, seq)
    tq = min(1024, seq)
    tk = min(2048, seq)
    return tm, tm_ffn, tq, tk


def _trunk(x, weights, na_bias, na_shift):
    stacked, per_layer = weights
    seq = x.shape[1]
    tm, tm_ffn, tq, tk = _tiles(seq)
    cos, sin, cost, sint = _rope_tables(seq)
    gsum = _group_sum_matrix()
    na_static = na_shift <= MAX_STATIC_SHIFT
    bias_blocks = _na_bias_blocks(na_bias, seq // GRID_W, jnp.where(na_static, na_shift, 0.0))
    for l, w in enumerate(per_layer):
        lambda_init = 0.8 - 0.6 * math.exp(-0.3 * l)
        qt, k, vt, qnt, kn, vnt = _proj(x, l, w["g_mix"], stacked["w_in"], w["gq_a"], w["gk_a"], w["gq_n"],
                                        w["gk_n"], cos, sin, cost, sint, gsum, tm)
        oa = _diff_attention(w["shift"], qt, k, vt, w["lam_q1"], w["lam_k1"], w["lam_q2"], w["lam_k2"], w["g_sub"],
                             lambda_init, tq, tk)
        on = _na_attention(na_static[l], l, qnt, kn, vnt, bias_blocks)
        x = _merge(x, oa, on, l, w["g_mix"], stacked["w_in"], stacked["w_pa"], stacked["w_pb"], stacked["w_o"], tm)
        x = _ffn(x, l, w["g_ffn"], stacked["w_up"], stacked["conv_w"], stacked["conv_b"], stacked["w_down"],
                 tm_ffn)
    return x


def _score_bound(gq, gk):
    bound = (HEAD_DIM ** 0.5) * LOG2E * jnp.max(jnp.abs(gq)) * jnp.max(jnp.abs(gk))
    return jnp.ceil(1.02 * bound.astype(F32)).reshape(1)


def _na_score_bound(gq, gk, na_bias):
    amax = lambda a: jnp.max(jnp.abs(a.astype(F32)).reshape(a.shape[0], -1), axis=1)
    bound = LOG2E * ((HEAD_DIM ** 0.5) * amax(gq) * amax(gk) + amax(na_bias))
    return jnp.ceil(1.02 * bound)


def _layer_weights(g_mix, w_in, gq_a, gk_a, lam_q1, lam_k1, lam_q2, lam_k2, g_sub, gq_n, gk_n,
                   w_pa, w_pb, w_o, g_ffn, w_up, conv_w, conv_b, w_down):
    depth = w_in.shape[0]
    stacked = dict(w_in=w_in.astype(BF16), w_pa=w_pa.astype(BF16), w_pb=w_pb.astype(BF16), w_o=w_o.astype(BF16),
                   w_up=w_up.astype(BF16), w_down=w_down.astype(BF16), conv_w=conv_w.astype(F32),
                   conv_b=conv_b.astype(F32)[:, None, :])
    head_tile = lambda g: jnp.tile(g.astype(F32), W_DIFF // HEAD_DIM)[None, :]
    row = lambda g: g.astype(F32)[None, :]
    col = lambda g: g.astype(F32)[:, None]
    per_layer = []
    for l in range(depth):
        per_layer.append(dict(
            g_mix=row(g_mix[l]), gq_a=col(gq_a[l]), gk_a=head_tile(gk_a[l]), gq_n=col(gq_n[l]),
            gk_n=head_tile(gk_n[l]),
            lam_q1=row(lam_q1[l]), lam_k1=row(lam_k1[l]), lam_q2=row(lam_q2[l]), lam_k2=row(lam_k2[l]),
            g_sub=col(g_sub[l]), shift=_score_bound(gq_a[l], gk_a[l]), g_ffn=row(g_ffn[l])))
    return stacked, per_layer


def kernel(x_prompt, x_sample, g_mix, w_in, gq_a, gk_a, lam_q1, lam_k1, lam_q2, lam_k2, g_sub, gq_n, gk_n,
           na_bias, w_pa, w_pb, w_o, g_ffn, w_up, conv_w, conv_b, w_down):
    weights = _layer_weights(g_mix, w_in, gq_a, gk_a, lam_q1, lam_k1, lam_q2, lam_k2, g_sub, gq_n, gk_n,
                             w_pa, w_pb, w_o, g_ffn, w_up, conv_w, conv_b, w_down)
    na_shift = _na_score_bound(gq_n, gk_n, na_bias)
    return tuple(_trunk(x, weights, na_bias, na_shift) for x in (x_prompt, x_sample))
```

```python
import functools
import math

import numpy as np
import jax
import jax.numpy as jnp
from jax import lax
from jax.experimental import pallas as pl
from jax.experimental.pallas import tpu as pltpu

D_MODEL = 1024
HEAD_DIM = 64
N_HEADS_DIFF = 4
N_HEADS_NA = 8
W_DIFF = N_HEADS_DIFF * 2 * HEAD_DIM
W_NA = N_HEADS_NA * HEAD_DIM
ATT_COLS = 3 * W_DIFF + 3 * W_NA
GRID_W = 64
WIN_H = 8
WIN_W = 16
D_FF = 2816
ROPE_THETA = 10000.0
EPS = 1e-6
NEG = -1e30

NA_QROWS = 4
NA_KROWS = 12
NA_SUBBLOCKS = 4
NA_LOOKAHEAD = 2
FFN_CHUNKS = (2304, 512)
MERGE_CHUNKS = 2
HALO = 8

VMEM_LIMIT = 56 * 1024 * 1024

F32 = jnp.float32
BF16 = jnp.bfloat16


def _const_spec(shape, index=None):
    index = (0,) * len(shape) if index is None else tuple(index)
    return pl.BlockSpec(shape, lambda *_: index, pipeline_mode=pl.Buffered(1))


def _params(sem):
    return pltpu.CompilerParams(dimension_semantics=sem, vmem_limit_bytes=VMEM_LIMIT)


def _rms(x, g):
    return x * lax.rsqrt(jnp.mean(x * x, axis=-1, keepdims=True) + EPS) * g


def _proj_kernel(x_ref, gmix_ref, w_ref, gqa_ref, gka_ref, gqn_ref, gkn_ref, cos_ref, sin_ref, cost_ref, sint_ref,
                 gsum_ref, qt_ref, k_ref, vt_ref, qnt_ref, kn_ref, vnt_ref):
    x = x_ref[0]
    h = _rms(x, gmix_ref[...]).astype(BF16)
    gsum = gsum_ref[...]
    half = HEAD_DIM // 2

    def proj(c0):
        return jnp.dot(h, w_ref[:, c0:c0 + W_DIFF], preferred_element_type=F32)

    def head_norm_t(zt, gcol):
        heads = []
        for r in range(0, W_DIFF, HEAD_DIM):
            blk = zt[r:r + HEAD_DIM]
            heads.append(blk * lax.rsqrt(jnp.mean(blk * blk, axis=0, keepdims=True) + EPS) * gcol)
        return heads

    def rope_t(blk):
        x1, x2 = blk[:half], blk[half:]
        return jnp.concatenate([x1 * cost_ref[...] - x2 * sint_ref[...], x1 * sint_ref[...] + x2 * cost_ref[...]],
                               axis=0)

    def head_norm(z, g):
        z2 = z * z
        hi = z2.astype(BF16)
        lo = (z2 - hi.astype(F32)).astype(BF16)
        parts = []
        for c in range(0, W_DIFF, 256):
            parts.append(jnp.dot(hi[:, c:c + 256], gsum, preferred_element_type=F32)
                         + jnp.dot(lo[:, c:c + 256], gsum, preferred_element_type=F32))
        ss = jnp.concatenate(parts, axis=-1)
        return z * lax.rsqrt(ss * (1.0 / HEAD_DIM) + EPS) * g

    cos = jnp.concatenate([cos_ref[...]] * 4, axis=-1)
    sin = jnp.concatenate([sin_ref[...]] * 4, axis=-1)
    lane = lax.broadcasted_iota(jnp.int32, (1, W_DIFF), 1)
    first_half = (lane % HEAD_DIM) < (HEAD_DIM // 2)

    def rope(y):
        partner = jnp.where(first_half, pltpu.roll(y, W_DIFF - HEAD_DIM // 2, 1), pltpu.roll(y, HEAD_DIM // 2, 1))
        return y * cos + partner * sin

    scale = HEAD_DIM ** -0.5 * LOG2E

    def emit_qa(z):
        qa = [rope_t(blk) for blk in head_norm_t(z.T, gqa_ref[...])]
        qt_ref[0] = (jnp.concatenate(qa, axis=0) * scale).astype(BF16)

    def emit_ka(z):
        k_ref[0] = rope(head_norm(z, gka_ref[...])).astype(BF16)

    def emit_va(z):
        vt_ref[0] = z.T.astype(BF16)

    def emit_qn(z):
        qnt_ref[0] = (jnp.concatenate(head_norm_t(z.T, gqn_ref[...]), axis=0) * scale).astype(BF16)

    def emit_kn(z):
        kn_ref[0] = head_norm(z, gkn_ref[...]).astype(BF16)

    def emit_vn(z):
        vnt_ref[0] = z.T.astype(BF16)

    emitters = (emit_qa, emit_ka, emit_va, emit_qn, emit_kn, emit_vn)
    z_next = proj(0)
    for i, emit in enumerate(emitters):
        z = z_next
        if i + 1 < len(emitters):
            z_next = proj((i + 1) * W_DIFF)
        emit(z)


def _proj(x, layer, gmix, w_in, gqa, gka, gqn, gkn, cos, sin, cost, sint, gsum, tm):
    b, s, _ = x.shape
    tok = lambda i, j: (i, j, 0)
    tr = lambda i, j: (i, 0, j)
    row_major = jax.ShapeDtypeStruct((b, s, W_DIFF), BF16)
    col_major = jax.ShapeDtypeStruct((b, W_DIFF, s), BF16)
    half = HEAD_DIM // 2
    return pl.pallas_call(
        _proj_kernel,
        grid=(b, s // tm),
        in_specs=[
            pl.BlockSpec((1, tm, D_MODEL), tok),
            _const_spec((1, D_MODEL)),
            _const_spec((None, D_MODEL, ATT_COLS), (layer, 0, 0)),
            _const_spec((HEAD_DIM, 1)), _const_spec((1, W_DIFF)), _const_spec((HEAD_DIM, 1)), _const_spec((1, W_NA)),
            pl.BlockSpec((tm, 128), lambda i, j: (j, 0)),
            pl.BlockSpec((tm, 128), lambda i, j: (j, 0)),
            pl.BlockSpec((half, tm), lambda i, j: (0, j)),
            pl.BlockSpec((half, tm), lambda i, j: (0, j)),
            _const_spec((256, 256)),
        ],
        out_specs=[
            pl.BlockSpec((1, W_DIFF, tm), tr),
            pl.BlockSpec((1, tm, W_DIFF), tok),
            pl.BlockSpec((1, W_DIFF, tm), tr),
            pl.BlockSpec((1, W_NA, tm), tr),
            pl.BlockSpec((1, tm, W_NA), tok),
            pl.BlockSpec((1, W_NA, tm), tr),
        ],
        out_shape=[col_major, row_major, col_major, col_major, row_major, col_major],
        compiler_params=_params(("parallel", "parallel")),
        name="proj",
    )(x, gmix, w_in, gqa, gka, gqn, gkn, cos, sin, cost, sint, gsum)


N_MAPS = 2 * N_HEADS_DIFF
LOG2E = math.log2(math.e)
MAX_STATIC_SHIFT = 60.0
DIFF_KEY_CHUNK = 2048
DIFF_LOOKAHEAD = 2


def _masked_q_group(qt_ref, j, row, lanes=slice(None)):
    grp = qt_ref[0, 128 * (j // 2):128 * (j // 2) + 128, lanes]
    keep = (row >= HEAD_DIM) if (j % 2) else (row < HEAD_DIM)
    return jnp.where(keep, grp, jnp.zeros_like(grp))


def _diff_finish(acc_sc, l_of, lq1_ref, lk1_ref, lq2_ref, lk2_ref, gsub_ref, o_ref, lambda_init):
    lam = (jnp.exp(jnp.sum(lq1_ref[...] * lk1_ref[...])) - jnp.exp(jnp.sum(lq2_ref[...] * lk2_ref[...]))
           + lambda_init)
    for h in range(N_HEADS_DIFF):
        o0 = acc_sc[h] * (1.0 / l_of(h))
        o1 = acc_sc[N_HEADS_DIFF + h] * (lam / l_of(N_HEADS_DIFF + h))
        o = o0 - o1
        o = o * (lax.rsqrt(jnp.mean(o * o, axis=0, keepdims=True) + EPS) * (1.0 - lambda_init)) * gsub_ref[...]
        o_ref[0, :, 128 * h:128 * h + 128] = o.T.astype(o_ref.dtype)


def _diff_online_kernel(shift_ref, qt_ref, k_ref, vt_ref, lq1_ref, lk1_ref, lq2_ref, lk2_ref, gsub_ref, o_ref,
                        qz_sc, m_sc, l_sc, acc_sc, *, lambda_init):
    del shift_ref
    kv = pl.program_id(2)
    tq = qt_ref.shape[2]

    @pl.when(kv == 0)
    def _init():
        m_sc[...] = jnp.full(m_sc.shape, NEG, F32)
        l_sc[...] = jnp.zeros(l_sc.shape, F32)
        acc_sc[...] = jnp.zeros(acc_sc.shape, F32)
        row = lax.broadcasted_iota(jnp.int32, (128, tq), 0)
        for j in range(N_MAPS):
            qz_sc[j] = _masked_q_group(qt_ref, j, row)

    for j in range(N_MAPS):
        h = j % N_HEADS_DIFF
        kp = k_ref[0, :, 128 * (j // 2):128 * (j // 2) + 128]
        s = jnp.dot(kp, qz_sc[j], preferred_element_type=F32)
        m_prev = m_sc[j]
        m_cur = jnp.maximum(m_prev, jnp.max(s, axis=0, keepdims=True))
        alpha = jnp.exp2(m_prev - m_cur)
        p = jnp.exp2(s - m_cur)
        l_sc[j] = alpha * l_sc[j] + jnp.sum(p, axis=0, keepdims=True)
        vt = vt_ref[0, 128 * h:128 * h + 128, :]
        acc_sc[j] = alpha * acc_sc[j] + jnp.dot(vt, p.astype(BF16), preferred_element_type=F32)
        m_sc[j] = m_cur

    @pl.when(kv == pl.num_programs(2) - 1)
    def _finish():
        _diff_finish(acc_sc, lambda j: l_sc[j], lq1_ref, lk1_ref, lq2_ref, lk2_ref, gsub_ref, o_ref, lambda_init)


def _diff_static_kernel(shift_ref, qt_ref, k_ref, vt_ref, lq1_ref, lk1_ref, lq2_ref, lk2_ref, gsub_ref, o_ref,
                        qz_sc, l_sc, acc_sc, *, lambda_init):
    kv = pl.program_id(2)
    tq = qt_ref.shape[2]
    tk = k_ref.shape[1]

    @pl.when(kv == 0)
    def _init():
        l_sc[...] = jnp.zeros(l_sc.shape, F32)
        acc_sc[...] = jnp.zeros(acc_sc.shape, F32)
        row = lax.broadcasted_iota(jnp.int32, (128, tq), 0)
        shift_rows = jnp.where(row == 0, -shift_ref[0], 0.0).astype(BF16)
        for j in range(N_MAPS):
            qz_sc[j, :128, :] = _masked_q_group(qt_ref, j, row)
            qz_sc[j, 128:, :] = shift_rows

    tc = min(DIFF_KEY_CHUNK, tk)
    ones = jnp.ones((tc, 128), BF16)
    units = [(j, c) for j in range(N_MAPS) for c in range(0, tk, tc)]

    def scores(j, c):
        kaug = jnp.concatenate([k_ref[0, c:c + tc, 128 * (j // 2):128 * (j // 2) + 128], ones], axis=1)
        return jnp.dot(kaug, qz_sc[j], preferred_element_type=F32)

    pending = [scores(*unit) for unit in units[:DIFF_LOOKAHEAD]]
    for u, (j, c) in enumerate(units):
        h = j % N_HEADS_DIFF
        s = pending.pop(0)
        if u + DIFF_LOOKAHEAD < len(units):
            pending.append(scores(*units[u + DIFF_LOOKAHEAD]))
        p = jnp.exp2(s)
        l_sc[j] += jnp.sum(p.reshape(tc // 8, 8, tq), axis=0)
        vt = vt_ref[0, 128 * h:128 * h + 128, c:c + tc]
        acc_sc[j] += jnp.dot(vt, p.astype(BF16), preferred_element_type=F32)

    @pl.when(kv == pl.num_programs(2) - 1)
    def _finish():
        _diff_finish(acc_sc, lambda j: jnp.sum(l_sc[j], axis=0, keepdims=True),
                     lq1_ref, lk1_ref, lq2_ref, lk2_ref, gsub_ref, o_ref, lambda_init)


def _diff_call(body, scratch, operands, lambda_init, tq, tk):
    qt = operands[1]
    b, _, s = qt.shape
    return pl.pallas_call(
        functools.partial(body, lambda_init=lambda_init),
        grid=(b, s // tq, s // tk),
        in_specs=[
            pl.BlockSpec(memory_space=pltpu.SMEM),
            pl.BlockSpec((1, W_DIFF, tq), lambda i, j, t: (i, 0, j)),
            pl.BlockSpec((1, tk, W_DIFF), lambda i, j, t: (i, t, 0)),
            pl.BlockSpec((1, W_DIFF, tk), lambda i, j, t: (i, 0, t)),
            _const_spec((1, HEAD_DIM)), _const_spec((1, HEAD_DIM)), _const_spec((1, HEAD_DIM)),
            _const_spec((1, HEAD_DIM)),
            _const_spec((2 * HEAD_DIM, 1)),
        ],
        out_specs=pl.BlockSpec((1, tq, W_DIFF), lambda i, j, t: (i, j, 0)),
        out_shape=jax.ShapeDtypeStruct((b, s, W_DIFF), BF16),
        scratch_shapes=scratch,
        compiler_params=_params(("parallel", "parallel", "arbitrary")),
        name=body.__name__.strip("_"),
    )(*operands)


def _diff_attention(shift, qt, k, vt, lq1, lk1, lq2, lk2, gsub, lambda_init, tq, tk):
    operands = (shift, qt, k, vt, lq1, lk1, lq2, lk2, gsub)
    static = functools.partial(
        _diff_call, _diff_static_kernel,
        [pltpu.VMEM((N_MAPS, 256, tq), BF16), pltpu.VMEM((N_MAPS, 8, tq), F32), pltpu.VMEM((N_MAPS, 128, tq), F32)],
        lambda_init=lambda_init, tq=tq, tk=tk)
    online = functools.partial(
        _diff_call, _diff_online_kernel,
        [pltpu.VMEM((N_MAPS, 128, tq), BF16), pltpu.VMEM((N_MAPS, 1, tq), F32), pltpu.VMEM((N_MAPS, 1, tq), F32),
         pltpu.VMEM((N_MAPS, 128, tq), F32)],
        lambda_init=lambda_init, tq=tq, tk=tk)
    return lax.cond(shift[0] <= MAX_STATIC_SHIFT, static, online, operands)


def _na_selectors(rows):
    kh = min(WIN_H, rows)
    assert kh == WIN_H and rows >= NA_KROWS + NA_QROWS
    rsel = np.zeros((3, NA_QROWS, NA_KROWS, 2 * WIN_H - 1), np.float32)
    for v in range(3):
        r0 = (0, NA_QROWS, rows - NA_QROWS)[v]
        ws = min(max(r0 - 4, 0), rows - NA_KROWS)
        for t in range(NA_QROWS):
            r = r0 + t
            rs = min(max(r - kh // 2, 0), rows - kh)
            for i in range(NA_KROWS):
                kr = ws + i
                if rs <= kr < rs + kh:
                    rsel[v, t, i, kr - r + WIN_H - 1] = 1.0
    csel = np.zeros((GRID_W, GRID_W, 2 * WIN_W - 1), np.float32)
    for c in range(GRID_W):
        cs = min(max(c - WIN_W // 2, 0), GRID_W - WIN_W)
        for j in range(cs, cs + WIN_W):
            csel[c, j, j - c + WIN_W - 1] = 1.0
    return rsel, csel


def _na_bias_blocks(na_bias, rows, shifts):
    rsel, csel = _na_selectors(rows)
    depth = na_bias.shape[0]
    n_rel = 2 * WIN_H
    slab = jnp.einsum("lhde,cje->lhdjc", na_bias.astype(F32), csel, precision=lax.Precision.HIGHEST)
    slab = slab * LOG2E - shifts.reshape(-1, 1, 1, 1, 1)
    slab = jnp.where((csel.sum(-1) > 0.5).T, slab, NEG)
    slab = jnp.concatenate([slab, jnp.full_like(slab[:, :, :1], NEG)], axis=2)
    rel_row = np.where(rsel.sum(-1) > 0.5, rsel.argmax(-1), n_rel - 1).astype(np.int32)

    per_step = 4

    def assemble(rel_ref, slab_ref, o_ref):
        v, i0 = pl.program_id(1), pl.program_id(2) * per_step
        for r in range(per_step):
            for t in range(NA_QROWS):
                d = rel_ref[(v * NA_QROWS + t) * NA_KROWS + i0 + r]
                o_ref[0, 0, :, GRID_W * r:GRID_W * (r + 1), GRID_W * t:GRID_W * (t + 1)] = slab_ref[0, :, d]

    return pl.pallas_call(
        assemble,
        grid_spec=pltpu.PrefetchScalarGridSpec(
            num_scalar_prefetch=1,
            grid=(depth, 3, NA_KROWS // per_step),
            in_specs=[pl.BlockSpec((1, N_HEADS_NA, n_rel, GRID_W, GRID_W), lambda l, v, i, rel: (l, 0, 0, 0, 0))],
            out_specs=pl.BlockSpec((1, 1, N_HEADS_NA, per_step * GRID_W, NA_QROWS * GRID_W),
                                   lambda l, v, i, rel: (l, v, 0, i, 0)),
        ),
        out_shape=jax.ShapeDtypeStruct((depth, 3, N_HEADS_NA, NA_KROWS * GRID_W, NA_QROWS * GRID_W), F32),
        compiler_params=_params(("parallel", "parallel", "parallel")),
        name="natt_bias_blocks",
    )(jnp.asarray(rel_row.reshape(-1)), slab)


def _na_kernel(qt_ref, *refs, subtract_max):
    n = NA_SUBBLOCKS
    k_refs, vt_refs = refs[:n], refs[n:2 * n]
    bias_first, bias_mid, bias_last, o_ref = refs[2 * n:]
    bias_refs = [bias_first] + [bias_mid] * (n - 2) + [bias_last]
    tq = NA_QROWS * GRID_W
    row = lax.broadcasted_iota(jnp.int32, (128, tq), 0)
    units = [(u, h) for u in range(n) for h in range(N_HEADS_NA)]

    def scores(u, h):
        q = _masked_q_group(qt_ref, h, row, slice(u * tq, (u + 1) * tq))
        return jnp.dot(k_refs[u][0, :, 128 * (h // 2):128 * (h // 2) + 128], q,
                       preferred_element_type=F32) + bias_refs[u][0, h]

    outs = []
    pending = [scores(*unit) for unit in units[:NA_LOOKAHEAD]]
    for i, (u, h) in enumerate(units):
        s = pending.pop(0)
        if i + NA_LOOKAHEAD < len(units):
            pending.append(scores(*units[i + NA_LOOKAHEAD]))
        if subtract_max:
            s = s - jnp.max(s, axis=0, keepdims=True)
        p = jnp.exp2(s)
        l = jnp.sum(p, axis=0, keepdims=True)
        vt = vt_refs[u][0, HEAD_DIM * h:HEAD_DIM * (h + 1), :]
        outs.append(jnp.dot(vt, p.astype(BF16), preferred_element_type=F32) * (1.0 / l))
        if h == N_HEADS_NA - 1:
            o_ref[0, u * tq:(u + 1) * tq, :] = jnp.concatenate(outs, axis=0).T.astype(o_ref.dtype)
            outs = []


def _na_call(subtract_max, layer, operands):
    qnt, kn, vnt, bias_blocks = operands
    b, s, _ = kn.shape
    rows = s // GRID_W
    nblk = rows // NA_QROWS
    n = NA_SUBBLOCKS
    tq = NA_QROWS * GRID_W
    win = NA_KROWS * GRID_W
    first = lambda g: jnp.clip(g - 1, 0, nblk - 3) * tq
    variant = lambda g: jnp.where(g == 0, 0, jnp.where(g == nblk - 1, 2, 1))
    kblk = lambda u: pl.BlockSpec((pl.Element(1), pl.Element(win), pl.Element(W_NA)),
                                  lambda i, j: (i, first(n * j + u), 0))
    vblk = lambda u: pl.BlockSpec((pl.Element(1), pl.Element(W_NA), pl.Element(win)),
                                  lambda i, j: (i, 0, first(n * j + u)))
    bias_shape = (None, 1, N_HEADS_NA, win, tq)
    bias_specs = [pl.BlockSpec(bias_shape, lambda i, j: (layer, variant(n * j), 0, 0, 0)),
                  _const_spec(bias_shape, (layer, 1, 0, 0, 0)),
                  pl.BlockSpec(bias_shape, lambda i, j: (layer, variant(n * j + n - 1), 0, 0, 0))]
    assert n >= 2 and nblk % n == 0
    subs = range(n)
    return pl.pallas_call(
        functools.partial(_na_kernel, subtract_max=subtract_max),
        grid=(b, nblk // n),
        in_specs=[pl.BlockSpec((1, W_NA, n * tq), lambda i, j: (i, 0, j))]
        + [kblk(u) for u in subs] + [vblk(u) for u in subs] + bias_specs,
        out_specs=pl.BlockSpec((1, n * tq, W_NA), lambda i, j: (i, j, 0)),
        out_shape=jax.ShapeDtypeStruct((b, s, W_NA), BF16),
        compiler_params=_params(("parallel", "arbitrary")),
        name="natt_online" if subtract_max else "natt_static",
    )(qnt, *([kn] * n), *([vnt] * n), bias_blocks, bias_blocks, bias_blocks)


def _na_attention(shift_is_static, layer, qnt, kn, vnt, bias_blocks):
    return lax.cond(shift_is_static, functools.partial(_na_call, False, layer),
                    functools.partial(_na_call, True, layer), (qnt, kn, vnt, bias_blocks))


def _merge_kernel(x_ref, oa_ref, on_ref, gmix_ref, wga_ref, wgb_ref, wpa_ref, wpb_ref, wo_ref, y_ref):
    x = x_ref[0]
    h = _rms(x, gmix_ref[...]).astype(BF16)
    oa, on = oa_ref[0], on_ref[0]
    n = D_MODEL // MERGE_CHUNKS

    def parts(c):
        return (jnp.dot(h, wga_ref[:, c:c + n], preferred_element_type=F32),
                jnp.dot(h, wgb_ref[:, c:c + n], preferred_element_type=F32),
                jnp.dot(oa, wpa_ref[:, c:c + n], preferred_element_type=F32),
                jnp.dot(on, wpb_ref[:, c:c + n], preferred_element_type=F32))

    acc = x
    nxt = parts(0)
    for i in range(MERGE_CHUNKS):
        ga, gb, pa, pb = nxt
        if i + 1 < MERGE_CHUNKS:
            nxt = parts((i + 1) * n)
        mixed = jax.nn.sigmoid(ga) * pa + jax.nn.sigmoid(gb) * pb
        acc = acc + jnp.dot(mixed.astype(BF16), wo_ref[i * n:(i + 1) * n, :], preferred_element_type=F32)
    y_ref[0] = acc


def _merge(x, oa, on, layer, gmix, w_in, w_pa, w_pb, w_o, tm):
    b, s, _ = x.shape
    tok = lambda i, j: (i, j, 0)
    gate_block = ATT_COLS // D_MODEL
    return pl.pallas_call(
        _merge_kernel,
        grid=(b, s // tm),
        in_specs=[
            pl.BlockSpec((1, tm, D_MODEL), tok),
            pl.BlockSpec((1, tm, W_DIFF), tok),
            pl.BlockSpec((1, tm, W_NA), tok),
            _const_spec((1, D_MODEL)),
            _const_spec((None, D_MODEL, D_MODEL), (layer, 0, gate_block)),
            _const_spec((None, D_MODEL, D_MODEL), (layer, 0, gate_block + 1)),
            _const_spec((None, W_DIFF, D_MODEL), (layer, 0, 0)),
            _const_spec((None, W_NA, D_MODEL), (layer, 0, 0)),
            _const_spec((None, D_MODEL, D_MODEL), (layer, 0, 0)),
        ],
        out_specs=pl.BlockSpec((1, tm, D_MODEL), tok),
        out_shape=jax.ShapeDtypeStruct(x.shape, F32),
        compiler_params=_params(("parallel", "parallel")),
        name="merge",
    )(x, oa, on, gmix, w_in, w_in, w_pa, w_pb, w_o)


def _ffn_kernel(x_ref, xp_ref, xn_ref, g_ref, wup_ref, cw_ref, cb_ref, wdn_ref, y_ref):
    j = pl.program_id(1)
    tm = x_ref.shape[1]
    x = x_ref[0]
    xe = jnp.concatenate([xp_ref[0], x, xn_ref[0]], axis=0)
    row = lax.broadcasted_iota(jnp.int32, (tm + 2 * HALO, 1), 0)
    inside = jnp.logical_and(jnp.logical_or(row >= HALO, j > 0),
                             jnp.logical_or(row < tm + HALO, j < pl.num_programs(1) - 1))
    he = jnp.where(inside, _rms(xe, g_ref[...]), 0.0).astype(BF16)

    def up(c0, n):
        return tuple(jnp.dot(he, wup_ref[:, c:c + n], preferred_element_type=F32)
                     for c in (c0, D_FF + c0))

    def conv(u, c0, n):
        w = cw_ref[:, c0:c0 + n]
        prev = pltpu.roll(u, 1, 0)[HALO:HALO + tm]
        nxt = pltpu.roll(u, tm + 2 * HALO - 1, 0)[HALO:HALO + tm]
        return prev * w[0:1] + u[HALO:HALO + tm] * w[1:2] + nxt * w[2:3] + cb_ref[:, c0:c0 + n]

    def gelu(v):
        c = math.sqrt(2.0 / math.pi)
        half_v = 0.5 * v
        return half_v + half_v * jnp.tanh(v * (c + (c * 0.044715) * (v * v)))

    starts = [sum(FFN_CHUNKS[:i]) for i in range(len(FFN_CHUNKS))]
    acc = x
    u_next = up(starts[0], FFN_CHUNKS[0])
    for i, (c0, n) in enumerate(zip(starts, FFN_CHUNKS)):
        ug, uv = u_next
        if i + 1 < len(FFN_CHUNKS):
            u_next = up(starts[i + 1], FFN_CHUNKS[i + 1])
        act = (gelu(conv(ug, c0, n)) * conv(uv, D_FF + c0, n)).astype(BF16)
        acc = acc + jnp.dot(act, wdn_ref[c0:c0 + n, :], preferred_element_type=F32)
    y_ref[0] = acc


def _ffn(x, layer, g, w_up, conv_w, conv_b, w_down, tm):
    b, s, _ = x.shape
    nb = tm // HALO
    last = s // HALO - 1
    return pl.pallas_call(
        _ffn_kernel,
        grid=(b, s // tm),
        in_specs=[
            pl.BlockSpec((1, tm, D_MODEL), lambda i, j: (i, j, 0)),
            pl.BlockSpec((1, HALO, D_MODEL), lambda i, j: (i, jnp.maximum(j * nb - 1, 0), 0)),
            pl.BlockSpec((1, HALO, D_MODEL), lambda i, j: (i, jnp.minimum((j + 1) * nb, last), 0)),
            _const_spec((1, D_MODEL)),
            _const_spec((None, D_MODEL, 2 * D_FF), (layer, 0, 0)),
            _const_spec((None, 3, 2 * D_FF), (layer, 0, 0)),
            _const_spec((None, 1, 2 * D_FF), (layer, 0, 0)),
            _const_spec((None, D_FF, D_MODEL), (layer, 0, 0)),
        ],
        out_specs=pl.BlockSpec((1, tm, D_MODEL), lambda i, j: (i, j, 0)),
        out_shape=jax.ShapeDtypeStruct(x.shape, F32),
        compiler_params=_params(("parallel", "parallel")),
        name="ffn",
    )(x, x, x, g, w_up, conv_w, conv_b, w_down)


def _rope_tables(seq):
    half = HEAD_DIM // 2
    inv = jnp.power(ROPE_THETA, -jnp.arange(half, dtype=F32) * 2.0 / HEAD_DIM)
    ang = jnp.arange(seq, dtype=F32)[:, None] * inv[None, :]
    cos, sin = jnp.cos(ang), jnp.sin(ang)
    return jnp.tile(cos, (1, 4)), jnp.tile(jnp.concatenate([-sin, sin], axis=-1), (1, 2)), cos.T, sin.T


def _group_sum_matrix():
    idx = np.arange(256) // HEAD_DIM
    return jnp.asarray(idx[:, None] == idx[None, :], dtype=BF16)


def _tiles(seq):
    tm = min(1024, seq)
    tm_ffn = min(512, seq)
    tq = min(1024, seq)
    tk = min(2048, seq)
    return tm, tm_ffn, tq, tk


def _trunk(x, weights, na_bias, na_shift):
    stacked, per_layer = weights
    seq = x.shape[1]
    tm, tm_ffn, tq, tk = _tiles(seq)
    cos, sin, cost, sint = _rope_tables(seq)
    gsum = _group_sum_matrix()
    na_static = na_shift <= MAX_STATIC_SHIFT
    bias_blocks = _na_bias_blocks(na_bias, seq // GRID_W, jnp.where(na_static, na_shift, 0.0))
    for l, w in enumerate(per_layer):
        lambda_init = 0.8 - 0.6 * math.exp(-0.3 * l)
        qt, k, vt, qnt, kn, vnt = _proj(x, l, w["g_mix"], stacked["w_in"], w["gq_a"], w["gk_a"], w["gq_n"],
                                        w["gk_n"], cos, sin, cost, sint, gsum, tm)
        oa = _diff_attention(w["shift"], qt, k, vt, w["lam_q1"], w["lam_k1"], w["lam_q2"], w["lam_k2"], w["g_sub"],
                             lambda_init, tq, tk)
        on = _na_attention(na_static[l], l, qnt, kn, vnt, bias_blocks)
        x = _merge(x, oa, on, l, w["g_mix"], stacked["w_in"], stacked["w_pa"], stacked["w_pb"], stacked["w_o"], tm)
        x = _ffn(x, l, w["g_ffn"], stacked["w_up"], stacked["conv_w"], stacked["conv_b"], stacked["w_down"],
                 tm_ffn)
    return x


def _score_bound(gq, gk):
    bound = (HEAD_DIM ** 0.5) * LOG2E * jnp.max(jnp.abs(gq)) * jnp.max(jnp.abs(gk))
    return jnp.ceil(1.02 * bound.astype(F32)).reshape(1)


def _na_score_bound(gq, gk, na_bias):
    amax = lambda a: jnp.max(jnp.abs(a.astype(F32)).reshape(a.shape[0], -1), axis=1)
    bound = LOG2E * ((HEAD_DIM ** 0.5) * amax(gq) * amax(gk) + amax(na_bias))
    return jnp.ceil(1.02 * bound)


def _layer_weights(g_mix, w_in, gq_a, gk_a, lam_q1, lam_k1, lam_q2, lam_k2, g_sub, gq_n, gk_n,
                   w_pa, w_pb, w_o, g_ffn, w_up, conv_w, conv_b, w_down):
    depth = w_in.shape[0]
    stacked = dict(w_in=w_in.astype(BF16), w_pa=w_pa.astype(BF16), w_pb=w_pb.astype(BF16), w_o=w_o.astype(BF16),
                   w_up=w_up.astype(BF16), w_down=w_down.astype(BF16), conv_w=conv_w.astype(F32),
                   conv_b=conv_b.astype(F32)[:, None, :])
    head_tile = lambda g: jnp.tile(g.astype(F32), W_DIFF // HEAD_DIM)[None, :]
    row = lambda g: g.astype(F32)[None, :]
    col = lambda g: g.astype(F32)[:, None]
    per_layer = []
    for l in range(depth):
        per_layer.append(dict(
            g_mix=row(g_mix[l]), gq_a=col(gq_a[l]), gk_a=head_tile(gk_a[l]), gq_n=col(gq_n[l]),
            gk_n=head_tile(gk_n[l]),
            lam_q1=row(lam_q1[l]), lam_k1=row(lam_k1[l]), lam_q2=row(lam_q2[l]), lam_k2=row(lam_k2[l]),
            g_sub=col(g_sub[l]), shift=_score_bound(gq_a[l], gk_a[l]), g_ffn=row(g_ffn[l])))
    return stacked, per_layer


def kernel(x_prompt, x_sample, g_mix, w_in, gq_a, gk_a, lam_q1, lam_k1, lam_q2, lam_k2, g_sub, gq_n, gk_n,
           na_bias, w_pa, w_pb, w_o, g_ffn, w_up, conv_w, conv_b, w_down):
    weights = _layer_weights(g_mix, w_in, gq_a, gk_a, lam_q1, lam_k1, lam_q2, lam_k2, g_sub, gq_n, gk_n,
                             w_pa, w_pb, w_o, g_ffn, w_up, conv_w, conv_b, w_down)
    na_shift = _na_score_bound(gq_n, gk_n, na_bias)
    return tuple(_trunk(x, weights, na_bias, na_shift) for x in (x_prompt, x_sample))
```
